```python
import math
import jax, jax.numpy as jnp
from jax import lax
import numpy as np

D_MODEL = 1024
BATCH = 2
SEQ = 16384
DEPTH = 2

D_MIX = D_MODEL
D_CONV = D_MIX // 4
D_POOL = D_MIX // 4
D_ATTN = D_MIX // 2
N_HEADS = 4
HEAD_DIM = D_ATTN // (2 * N_HEADS)
V_DIM = 2 * HEAD_DIM
CONV_WIDTH = 31
POOL_WINDOWS = (2, 4, 8, 16)
N_POOL_GROUPS = len(POOL_WINDOWS)
POOL_GROUP_DIM = D_POOL // N_POOL_GROUPS
D_FF = 256 * ((8 * D_MODEL // 3 + 255) // 256)
D_IN = 2 * D_CONV + D_POOL + 3 * D_ATTN
Q_BLOCK = 128
NORM_EPS = 1e-6

kernel_name = "hybrid_conv_pool_diffattn_encoder"


def rms_norm(x, g):
    xf = x.astype(jnp.float32)
    y = xf * lax.rsqrt(jnp.mean(xf * xf, axis=-1, keepdims=True) + NORM_EPS)
    return (y * g.astype(jnp.float32)).astype(x.dtype)


def layer_norm(x, g, b):
    xf = x.astype(jnp.float32)
    mu = jnp.mean(xf, axis=-1, keepdims=True)
    xc = xf - mu
    y = xc * lax.rsqrt(jnp.mean(xc * xc, axis=-1, keepdims=True) + NORM_EPS)
    return (y * g.astype(jnp.float32) + b.astype(jnp.float32)).astype(x.dtype)


def swiglu_ffn(h, w_gate, w_up, w_down):
    return (jax.nn.silu(h @ w_gate) * (h @ w_up)) @ w_down


def conformer_conv(u, w_dw, b_dw, ln_g, ln_b):
    a, gate = jnp.split(u, 2, axis=-1)
    z = a * jax.nn.sigmoid(gate)
    z = lax.conv_general_dilated(
        z, w_dw[:, None, :], window_strides=(1,),
        padding=((CONV_WIDTH // 2, CONV_WIDTH // 2),),
        dimension_numbers=("NWC", "WIO", "NWC"),
        feature_group_count=D_CONV) + b_dw
    return jax.nn.silu(layer_norm(z, ln_g, ln_b))


def multiscale_pool(u, w_grp, scale):
    B_, S_, _ = u.shape
    uf = u.astype(jnp.float32)
    cs = jnp.concatenate([jnp.zeros((B_, 1, D_POOL), jnp.float32), jnp.cumsum(uf, axis=1)], axis=1)
    t = jnp.arange(S_, dtype=jnp.int32)
    outs = []
    for g, w in enumerate(POOL_WINDOWS):
        sl = slice(g * POOL_GROUP_DIM, (g + 1) * POOL_GROUP_DIM)
        lo = jnp.clip(t - w // 2, 0, S_)
        hi = jnp.clip(t + w // 2, 0, S_)
        csg = cs[..., sl]
        win_sum = jnp.take(csg, hi, axis=1) - jnp.take(csg, lo, axis=1)
        cnt = (hi - lo).astype(jnp.float32)[None, :, None]
        outs.append(win_sum / cnt - uf[..., sl])
    d = jnp.stack(outs, axis=2)
    y = jnp.einsum("bsgc,gcd->bsgd", d, w_grp.astype(jnp.float32)).reshape(B_, S_, D_POOL)
    return (y * scale.astype(jnp.float32)).astype(u.dtype)


def diff_attention(q, k, v, lam, slopes):
    B_, S_ = q.shape[0], q.shape[1]
    n_blk = S_ // Q_BLOCK
    q_blocks = jnp.moveaxis(q.reshape(B_, n_blk, Q_BLOCK, N_HEADS, 2, HEAD_DIM), 1, 0)
    k_pos = jnp.arange(S_, dtype=jnp.int32)

    def block(args):
        q_blk, blk = args
        q_pos = blk * Q_BLOCK + jnp.arange(Q_BLOCK, dtype=jnp.int32)
        dist = jnp.abs(q_pos[:, None] - k_pos[None, :]).astype(jnp.float32)
        s = jnp.einsum("bqhcd,bkhcd->bhcqk", q_blk, k, preferred_element_type=jnp.float32)
        s = s - slopes[None, :, None, None, None] * dist
        p = jax.nn.softmax(s, axis=-1)
        a = p[:, :, 0] - lam * p[:, :, 1]
        return jnp.einsum("bhqk,bkhe->bqhe", a.astype(v.dtype), v)

    o = lax.map(block, (q_blocks, jnp.arange(n_blk, dtype=jnp.int32)))
    return jnp.moveaxis(o, 0, 1).reshape(B_, S_, N_HEADS, V_DIM)


def setup_inputs(seed: int = 0) -> dict:
    key = jax.random.key(seed)
    ks = iter(jax.random.split(key, 32))

    def nrm(shape, scale):
        return jax.random.normal(next(ks), shape, jnp.float32) * scale

    def gain(shape):
        return 1.0 + 0.02 * jax.random.normal(next(ks), shape, jnp.float32)

    L = DEPTH
    return {
        "x": nrm((BATCH, SEQ, D_MODEL), 1.0),
        "ffn1_norm": gain((L, D_MODEL)),
        "ffn1_w_gate": nrm((L, D_MODEL, D_FF), D_MODEL ** -0.5),
        "ffn1_w_up": nrm((L, D_MODEL, D_FF), D_MODEL ** -0.5),
        "ffn1_w_down": nrm((L, D_FF, D_MODEL), D_FF ** -0.5),
        "mix_norm": gain((L, D_MODEL)),
        "w_in": nrm((L, D_MODEL, D_IN), D_MODEL ** -0.5),
        "conv_dw": nrm((L, CONV_WIDTH, D_CONV), CONV_WIDTH ** -0.5),
        "conv_dw_bias": nrm((L, D_CONV), 0.02),
        "conv_ln_gain": gain((L, D_CONV)),
        "conv_ln_bias": nrm((L, D_CONV), 0.02),
        "pool_w": nrm((L, N_POOL_GROUPS, POOL_GROUP_DIM, POOL_GROUP_DIM), POOL_GROUP_DIM ** -0.5),
        "pool_scale": 1.0 + 0.1 * jax.random.normal(next(ks), (L, D_POOL), jnp.float32),
        "q_norm": gain((L, HEAD_DIM)),
        "k_norm": gain((L, HEAD_DIM)),
        "lambda_q1": nrm((L, HEAD_DIM), 0.1),
        "lambda_k1": nrm((L, HEAD_DIM), 0.1),
        "lambda_q2": nrm((L, HEAD_DIM), 0.1),
        "lambda_k2": nrm((L, HEAD_DIM), 0.1),
        "attn_subln": gain((L, V_DIM)),
        "w_out": nrm((L, D_MIX, D_MODEL), D_MIX ** -0.5),
        "ffn2_norm": gain((L, D_MODEL)),
        "ffn2_w_gate": nrm((L, D_MODEL, D_FF), D_MODEL ** -0.5),
        "ffn2_w_up": nrm((L, D_MODEL, D_FF), D_MODEL ** -0.5),
        "ffn2_w_down": nrm((L, D_FF, D_MODEL), D_FF ** -0.5),
        "post_norm": gain((L, D_MODEL)),
    }


def reference(x, ffn1_norm, ffn1_w_gate, ffn1_w_up, ffn1_w_down, mix_norm, w_in,
              conv_dw, conv_dw_bias, conv_ln_gain, conv_ln_bias, pool_w, pool_scale,
              q_norm, k_norm, lambda_q1, lambda_k1, lambda_q2, lambda_k2, attn_subln,
              w_out, ffn2_norm, ffn2_w_gate, ffn2_w_up, ffn2_w_down, post_norm):
    B_, S_ = x.shape[0], x.shape[1]
    slopes = jnp.exp2(-8.0 * jnp.arange(1, N_HEADS + 1, dtype=jnp.float32) / N_HEADS)
    for l in range(DEPTH):
        lambda_init = 0.8 - 0.6 * math.exp(-0.3 * l)
        x = x + 0.5 * swiglu_ffn(rms_norm(x, ffn1_norm[l]), ffn1_w_gate[l], ffn1_w_up[l], ffn1_w_down[l])
        h = rms_norm(x, mix_norm[l])
        u = h @ w_in[l]
        u_conv = u[..., :2 * D_CONV]
        u_pool = u[..., 2 * D_CONV:2 * D_CONV + D_POOL]
        q, k, v = jnp.split(u[..., 2 * D_CONV + D_POOL:], 3, axis=-1)
        y_conv = conformer_conv(u_conv, conv_dw[l], conv_dw_bias[l], conv_ln_gain[l], conv_ln_bias[l])
        y_pool = multiscale_pool(u_pool, pool_w[l], pool_scale[l])
        q = rms_norm(q.reshape(B_, S_, N_HEADS, 2, HEAD_DIM), q_norm[l]) * (HEAD_DIM ** -0.5)
        k = rms_norm(k.reshape(B_, S_, N_HEADS, 2, HEAD_DIM), k_norm[l])
        v = v.reshape(B_, S_, N_HEADS, V_DIM)
        lam = (jnp.exp(jnp.sum(lambda_q1[l].astype(jnp.float32) * lambda_k1[l].astype(jnp.float32)))
               - jnp.exp(jnp.sum(lambda_q2[l].astype(jnp.float32) * lambda_k2[l].astype(jnp.float32)))
               + lambda_init)
        o = diff_attention(q, k, v, lam, slopes)
        y_attn = (rms_norm(o, attn_subln[l]) * (1.0 - lambda_init)).reshape(B_, S_, D_ATTN)
        y = jnp.concatenate([y_conv, y_pool, y_attn.astype(y_conv.dtype)], axis=-1) @ w_out[l]
        x = x + y
        x = x + 0.5 * swiglu_ffn(rms_norm(x, ffn2_norm[l]), ffn2_w_gate[l], ffn2_w_up[l], ffn2_w_down[l])
        x = rms_norm(x, post_norm[l])
    return x
```

```python
import functools
import math

import jax
import jax.numpy as jnp
from jax import lax
from jax.experimental import pallas as pl
from jax.experimental.pallas import tpu as pltpu

F32 = jnp.float32
BF16 = jnp.bfloat16

NORM_EPS = 1e-6
N_HEADS = 4
HEAD_DIM = 64
V_DIM = 2 * HEAD_DIM
D_CONV = 256
D_POOL = 256
D_ATTN = 512
CONV_WIDTH = 31
POOL_GROUP_DIM = 64
HALO = 16
V_ROWS = V_DIM + 8
AUG = 4
BF16_EXACT_INT = 256
VMEM_LIMIT_BYTES = 56 * 1024 * 1024
NEG_BIG = -1e30


def _attn_block(seq):
    return min(512, seq)


def _row_block(rows):
    return min(256, rows)


def _rms(x, g):
    ms = jnp.mean(x * x, axis=-1, keepdims=True)
    return x * lax.rsqrt(ms + NORM_EPS) * g


def _sigmoid(x):
    return 1.0 / (1.0 + jnp.exp(-x))


def _swiglu(h, wg_ref, wu_ref, wd_ref):
    hb = h.astype(BF16)
    g = jnp.dot(hb, wg_ref[...], preferred_element_type=F32)
    u = jnp.dot(hb, wu_ref[...], preferred_element_type=F32)
    a = (g * _sigmoid(g) * u).astype(BF16)
    return jnp.dot(a, wd_ref[...], preferred_element_type=F32)


def _resident(shape):
    nd = len(shape)
    return pl.BlockSpec(shape, lambda *_: (0,) * nd, pipeline_mode=pl.Buffered(1))


def _ffn1_win_kernel(x_ref, g1_ref, wg_ref, wu_ref, wd_ref, gm_ref, win_ref,
                     x1_ref, ucp_ref, uq_ref, uk_ref, uv_ref):
    x = x_ref[...]
    x1 = x + 0.5 * _swiglu(_rms(x, g1_ref[...]), wg_ref, wu_ref, wd_ref)
    x1_ref[...] = x1
    hm = _rms(x1, gm_ref[...]).astype(BF16)
    u = jnp.dot(hm, win_ref[...], preferred_element_type=F32)
    c0 = 2 * D_CONV + D_POOL
    ucp_ref[...] = u[:, :c0]
    uq_ref[...] = u[:, c0:c0 + D_ATTN]
    uk_ref[...] = u[:, c0 + D_ATTN:c0 + 2 * D_ATTN]
    uv_ref[...] = u[:, c0 + 2 * D_ATTN:]


def _ffn1_win(x, g1, wg, wu, wd, gm, win):
    rows, d = x.shape
    f = wg.shape[1]
    tm = _row_block(rows)
    c0 = 2 * D_CONV + D_POOL
    row_spec = lambda w: pl.BlockSpec((tm, w), lambda i: (i, 0))
    return pl.pallas_call(
        _ffn1_win_kernel,
        grid=(rows // tm,),
        in_specs=[row_spec(d), _resident((1, d)), _resident((d, f)), _resident((d, f)),
                  _resident((f, d)), _resident((1, d)), _resident(win.shape)],
        out_specs=[row_spec(d), row_spec(c0), row_spec(D_ATTN), row_spec(D_ATTN), row_spec(D_ATTN)],
        out_shape=[jax.ShapeDtypeStruct((rows, d), F32),
                   jax.ShapeDtypeStruct((rows, c0), F32),
                   jax.ShapeDtypeStruct((rows, D_ATTN), F32),
                   jax.ShapeDtypeStruct((rows, D_ATTN), F32),
                   jax.ShapeDtypeStruct((rows, D_ATTN), F32)],
        compiler_params=pltpu.CompilerParams(dimension_semantics=("arbitrary",),
                                             vmem_limit_bytes=VMEM_LIMIT_BYTES),
        name="ffn1_win",
    )(x, g1, wg, wu, wd, gm, win)


def _convpool_kernel(seq, main_ref, prev_ref, next_ref, dw_ref, dwb_ref, lng_ref, lnb_ref,
                     pw_ref, ps_ref, out_ref, zext, pext, a1, a2, a3):
    ts = main_ref.shape[0]
    i = pl.program_id(1)
    has_prev = i > 0
    has_next = i < pl.num_programs(1) - 1

    def glu(u):
        return u[:, :D_CONV] * _sigmoid(u[:, D_CONV:2 * D_CONV])

    um = main_ref[...]
    up = prev_ref[...]
    un = next_ref[...]
    zero_halo = jnp.zeros((HALO, D_CONV), F32)
    zext[0:HALO, :] = jnp.where(has_prev, glu(up), zero_halo)
    zext[HALO:HALO + ts, :] = glu(um)
    zext[HALO + ts:, :] = jnp.where(has_next, glu(un), zero_halo)
    pext[0:HALO, :] = jnp.where(has_prev, up[:, 2 * D_CONV:], zero_halo)
    pext[HALO:HALO + ts, :] = um[:, 2 * D_CONV:]
    pext[HALO + ts:, :] = jnp.where(has_next, un[:, 2 * D_CONV:], zero_halo)

    off0 = HALO - CONV_WIDTH // 2
    acc = jnp.zeros((ts, D_CONV), F32)
    for j in range(CONV_WIDTH):
        acc = acc + dw_ref[j:j + 1, :] * zext[off0 + j:off0 + j + ts, :]
    z = acc + dwb_ref[...]
    mu = jnp.mean(z, axis=-1, keepdims=True)
    zc = z - mu
    var = jnp.mean(zc * zc, axis=-1, keepdims=True)
    y = zc * lax.rsqrt(var + NORM_EPS) * lng_ref[...] + lnb_ref[...]
    out_ref[:, :D_CONV] = (y * _sigmoid(y)).astype(out_ref.dtype)

    n = ts + 2 * HALO
    a1[1:n, :] = pext[0:n - 1, :] + pext[1:n, :]
    a2[2:n - 1, :] = a1[1:n - 2, :] + a1[3:n, :]
    a3[4:n - 3, :] = a2[2:n - 5, :] + a2[6:n - 1, :]
    w2 = a1[HALO:HALO + ts, :]
    w4 = a2[HALO:HALO + ts, :]
    w8 = a3[HALO:HALO + ts, :]
    w16 = a3[HALO - 4:HALO - 4 + ts, :] + a3[HALO + 4:HALO + 4 + ts, :]
    grp = lax.broadcasted_iota(jnp.int32, (ts, D_POOL), 1) // POOL_GROUP_DIM
    win = jnp.where(grp == 0, w2, jnp.where(grp == 1, w4, jnp.where(grp == 2, w8, w16)))
    t = lax.broadcasted_iota(jnp.int32, (ts, D_POOL), 0) + i * ts
    half = jnp.left_shift(1, grp)
    cnt = jnp.minimum(t + half, seq) - jnp.maximum(t - half, 0)
    d = win / cnt.astype(F32) - pext[HALO:HALO + ts, :]
    yp = jnp.dot(d.astype(BF16), pw_ref[...], preferred_element_type=F32) * ps_ref[...]
    out_ref[:, D_CONV:] = yp.astype(out_ref.dtype)


def _convpool(ucp, dw, dwb, lng, lnb, pw_bd, ps):
    b, seq, c0 = ucp.shape
    ts = _attn_block(seq)
    nh = ts // HALO
    last_halo = seq // HALO - 1
    n = ts + 2 * HALO
    return pl.pallas_call(
        functools.partial(_convpool_kernel, seq),
        grid=(b, seq // ts),
        in_specs=[pl.BlockSpec((None, ts, c0), lambda bi, i: (bi, i, 0)),
                  pl.BlockSpec((None, HALO, c0), lambda bi, i: (bi, jnp.maximum(i * nh - 1, 0), 0)),
                  pl.BlockSpec((None, HALO, c0),
                               lambda bi, i: (bi, jnp.minimum((i + 1) * nh, last_halo), 0)),
                  _resident(dw.shape), _resident(dwb.shape), _resident(lng.shape),
                  _resident(lnb.shape), _resident(pw_bd.shape), _resident(ps.shape)],
        out_specs=pl.BlockSpec((None, ts, D_CONV + D_POOL), lambda bi, i: (bi, i, 0)),
        out_shape=jax.ShapeDtypeStruct((b, seq, D_CONV + D_POOL), BF16),
        scratch_shapes=[pltpu.VMEM((n, D_CONV), F32), pltpu.VMEM((n, D_POOL), F32),
                        pltpu.VMEM((n, D_POOL), F32), pltpu.VMEM((n, D_POOL), F32),
                        pltpu.VMEM((n, D_POOL), F32)],
        compiler_params=pltpu.CompilerParams(dimension_semantics=("arbitrary", "arbitrary"),
                                             vmem_limit_bytes=VMEM_LIMIT_BYTES),
        name="convpool",
    )(ucp, ucp, ucp, dw, dwb, lng, lnb, pw_bd, ps)


def _slope(h):
    return 2.0 ** (-8.0 * (h + 1) / N_HEADS)


def _qkv_prep_kernel(blk, uq_ref, uk_ref, uv_ref, gq_ref, gk_ref, qT_ref, ka_ref, vT_ref):
    ts = uq_ref.shape[0]
    i = pl.program_id(1)
    lane = lax.broadcasted_iota(jnp.int32, (ts, V_DIM), 1)
    low = lane < HEAD_DIM
    pos = (lax.broadcasted_iota(jnp.int32, (ts, V_DIM), 0) + i * ts) % blk
    pos_lo = (pos % BF16_EXACT_INT).astype(F32)
    pos_hi = (pos - pos % BF16_EXACT_INT).astype(F32)
    one = jnp.ones((ts, V_DIM), F32)
    zero = jnp.zeros((ts, V_DIM), F32)
    row8 = lax.broadcasted_iota(jnp.int32, (V_ROWS - V_DIM, ts), 0)
    ones_rows = jnp.where(row8 == 0, 1.0, 0.0).astype(vT_ref.dtype)

    def norm_pair(x, g):
        x2 = x * x
        s_lo = jnp.sum(jnp.where(low, x2, 0.0), axis=-1, keepdims=True)
        s_hi = jnp.sum(jnp.where(low, 0.0, x2), axis=-1, keepdims=True)
        ms = jnp.where(low, s_lo, s_hi) * (1.0 / HEAD_DIM)
        return x * lax.rsqrt(ms + NORM_EPS) * g

    def aug(c0, c1, c2, c3):
        a = jnp.where(lane == HEAD_DIM, c0, zero)
        a = jnp.where(lane == HEAD_DIM + 1, c1, a)
        a = jnp.where(lane == HEAD_DIM + 2, c2, a)
        return jnp.where(lane == HEAD_DIM + 3, c3, a)

    for h in range(N_HEADS):
        sl = _slope(h)
        cols = slice(h * V_DIM, (h + 1) * V_DIM)
        qn = norm_pair(uq_ref[:, cols], gq_ref[...]) * (HEAD_DIM ** -0.5)
        kn = norm_pair(uk_ref[:, cols], gk_ref[...])
        q_aug = aug(one, one, sl * pos_lo, sl * pos_hi)
        k_aug = aug(-sl * pos_lo, -sl * pos_hi, one, one)
        for c in range(2):
            qc = qn if c == 0 else pltpu.roll(qn, HEAD_DIM, 1)
            kc = kn if c == 0 else pltpu.roll(kn, HEAD_DIM, 1)
            qT_ref[h, c] = jnp.where(low, qc, q_aug).T.astype(qT_ref.dtype)
            ka_ref[h, c] = jnp.where(low, kc, k_aug).astype(ka_ref.dtype)
        vT_ref[h, 0, :V_DIM, :] = uv_ref[:, cols].T.astype(vT_ref.dtype)
        vT_ref[h, 0, V_DIM:, :] = ones_rows


def _qkv_prep(uq, uk, uv, gq, gk):
    b, seq, _ = uq.shape
    blk = _attn_block(seq)
    nblk = seq // blk
    in_spec = pl.BlockSpec((None, blk, D_ATTN), lambda bi, i: (bi, i, 0))
    return pl.pallas_call(
        functools.partial(_qkv_prep_kernel, blk),
        grid=(b, nblk),
        in_specs=[in_spec, in_spec, in_spec, _resident(gq.shape), _resident(gk.shape)],
        out_specs=[pl.BlockSpec((None, N_HEADS, 2, V_DIM, blk), lambda bi, i: (bi, 0, 0, 0, i)),
                   pl.BlockSpec((None, N_HEADS, 2, blk, V_DIM), lambda bi, i: (bi, 0, 0, i, 0)),
                   pl.BlockSpec((None, N_HEADS, 1, V_ROWS, blk), lambda bi, i: (bi, 0, i, 0, 0))],
        out_shape=[jax.ShapeDtypeStruct((b, N_HEADS, 2, V_DIM, seq), BF16),
                   jax.ShapeDtypeStruct((b, N_HEADS, 2, seq, V_DIM), BF16),
                   jax.ShapeDtypeStruct((b, N_HEADS, nblk, V_ROWS, blk), BF16)],
        compiler_params=pltpu.CompilerParams(dimension_semantics=("arbitrary", "arbitrary"),
                                             vmem_limit_bytes=VMEM_LIMIT_BYTES),
        name="qkv_prep",
    )(uq, uk, uv, gq, gk)


def _attn_kernel(lambda_init, slopes_ref, qT_ref, ka_ref, vT_ref, lq1_ref, lk1_ref, lq2_ref,
                 lk2_ref, sub_ref, out_ref, qv_ref, m_ref, acc_ref):
    h = pl.program_id(1)
    i = pl.program_id(2)
    nk = ka_ref.shape[1]
    tk = ka_ref.shape[2]
    tq = qT_ref.shape[2]
    slope = slopes_ref[h]

    row = lax.broadcasted_iota(jnp.int32, (V_DIM, tq), 0)
    for c in range(2):
        q = qT_ref[c]
        qv_ref[0, c] = jnp.where(row < HEAD_DIM, q, -q)
        qv_ref[1, c] = jnp.where(row < HEAD_DIM, q, jnp.zeros_like(q))
    m_ref[...] = jnp.full(m_ref.shape, NEG_BIG, F32)
    acc_ref[...] = jnp.zeros(acc_ref.shape, F32)

    def step(j, q_of_map, diag_bias):
        dist = jnp.abs(j - i) * tk
        off = slope * jnp.full((1, tq), dist, jnp.int32).astype(F32)
        for c in range(2):
            s = jnp.dot(ka_ref[c, j], q_of_map(c), preferred_element_type=F32)
            if diag_bias is not None:
                s = s - diag_bias
            m_old = m_ref[c]
            m_new = jnp.maximum(m_old, jnp.max(s, axis=0, keepdims=True) - off)
            p = jnp.exp(s - (m_new + off)).astype(BF16)
            alpha = jnp.exp(m_old - m_new)
            acc_ref[c] = acc_ref[c] * alpha + jnp.dot(vT_ref[j], p, preferred_element_type=F32)
            m_ref[c] = m_new

    def below(j, carry):
        step(j, lambda c: qv_ref[0, c], None)
        return carry

    def above(j, carry):
        step(j, lambda c: qT_ref[c], None)
        return carry

    lax.fori_loop(0, i, below, 0)
    kk = lax.broadcasted_iota(jnp.int32, (tk, tq), 0)
    qq = lax.broadcasted_iota(jnp.int32, (tk, tq), 1)
    step(i, lambda c: qv_ref[1, c], slope * jnp.abs(kk - qq).astype(F32))
    lax.fori_loop(i + 1, nk, above, 0)

    lam = (jnp.exp(jnp.sum(lq1_ref[...] * lk1_ref[...], axis=-1, keepdims=True))
           - jnp.exp(jnp.sum(lq2_ref[...] * lk2_ref[...], axis=-1, keepdims=True))
           + lambda_init)
    a0 = acc_ref[0]
    a1 = acc_ref[1]
    o = a0[:V_DIM] / a0[V_DIM:V_DIM + 1] - lam * (a1[:V_DIM] / a1[V_DIM:V_DIM + 1])
    ms = jnp.mean(o * o, axis=0, keepdims=True)
    y = o * lax.rsqrt(ms + NORM_EPS) * sub_ref[...] * (1.0 - lambda_init)
    out_ref[...] = y.T.astype(out_ref.dtype)


def _attention(qT, ka, vT, lq1, lk1, lq2, lk2, sub_col, lambda_init):
    b, nh, _, _, seq = qT.shape
    blk = vT.shape[-1]
    nblk = seq // blk
    ka = ka.reshape(b, nh, 2, nblk, blk, V_DIM)
    slopes = jnp.asarray([_slope(h) for h in range(nh)], F32)
    return pl.pallas_call(
        functools.partial(_attn_kernel, lambda_init),
        grid=(b, nh, nblk),
        in_specs=[pl.BlockSpec(memory_space=pltpu.SMEM),
                  pl.BlockSpec((None, None, 2, V_DIM, blk), lambda bi, h, i: (bi, h, 0, 0, i)),
                  pl.BlockSpec((None, None, 2, nblk, blk, V_DIM), lambda bi, h, i: (bi, h, 0, 0, 0, 0)),
                  pl.BlockSpec((None, None, nblk, V_ROWS, blk), lambda bi, h, i: (bi, h, 0, 0, 0)),
                  _resident(lq1.shape), _resident(lk1.shape), _resident(lq2.shape),
                  _resident(lk2.shape), _resident(sub_col.shape)],
        out_specs=pl.BlockSpec((None, blk, V_DIM), lambda bi, h, i: (bi, i, h)),
        out_shape=jax.ShapeDtypeStruct((b, seq, nh * V_DIM), BF16),
        scratch_shapes=[pltpu.VMEM((2, 2, V_DIM, blk), BF16),
                        pltpu.VMEM((2, 1, blk), F32),
                        pltpu.VMEM((2, V_ROWS, blk), F32)],
        compiler_params=pltpu.CompilerParams(
            dimension_semantics=("arbitrary", "arbitrary", "arbitrary"),
            vmem_limit_bytes=VMEM_LIMIT_BYTES),
        name="attention",
    )(slopes, qT, ka, vT, lq1, lk1, lq2, lk2, sub_col)


def _out_ffn2_kernel(x1_ref, ycp_ref, yat_ref, wo_cp_ref, wo_at_ref, g2_ref, wg_ref, wu_ref,
                     wd_ref, gp_ref, out_ref):
    x2 = (x1_ref[...]
          + jnp.dot(ycp_ref[...], wo_cp_ref[...], preferred_element_type=F32)
          + jnp.dot(yat_ref[...], wo_at_ref[...], preferred_element_type=F32))
    x3 = x2 + 0.5 * _swiglu(_rms(x2, g2_ref[...]), wg_ref, wu_ref, wd_ref)
    out_ref[...] = _rms(x3, gp_ref[...])


def _out_ffn2(x1, ycp, yat, wo_cp, wo_at, g2, wg, wu, wd, gp):
    rows, d = x1.shape
    f = wg.shape[1]
    tm = _row_block(rows)
    row_spec = lambda w: pl.BlockSpec((tm, w), lambda i: (i, 0))
    return pl.pallas_call(
        _out_ffn2_kernel,
        grid=(rows // tm,),
        in_specs=[row_spec(d), row_spec(ycp.shape[1]), row_spec(yat.shape[1]),
                  _resident(wo_cp.shape), _resident(wo_at.shape), _resident((1, d)),
                  _resident((d, f)), _resident((d, f)), _resident((f, d)), _resident((1, d))],
        out_specs=row_spec(d),
        out_shape=jax.ShapeDtypeStruct((rows, d), F32),
        compiler_params=pltpu.CompilerParams(dimension_semantics=("arbitrary",),
                                             vmem_limit_bytes=VMEM_LIMIT_BYTES),
        name="out_ffn2",
    )(x1, ycp, yat, wo_cp, wo_at, g2, wg, wu, wd, gp)


def _pool_block_diag(pool_w):
    g, cg, _ = pool_w.shape
    out = jnp.zeros((g * cg, g * cg), pool_w.dtype)
    for k in range(g):
        out = out.at[k * cg:(k + 1) * cg, k * cg:(k + 1) * cg].set(pool_w[k])
    return out


def kernel(x, ffn1_norm, ffn1_w_gate, ffn1_w_up, ffn1_w_down, mix_norm, w_in, conv_dw, conv_dw_bias, conv_ln_gain, conv_ln_bias, pool_w, pool_scale, q_norm, k_norm, lambda_q1, lambda_k1, lambda_q2, lambda_k2, attn_subln, w_out, ffn2_norm, ffn2_w_gate, ffn2_w_up, ffn2_w_down, post_norm):
    b, seq, d = x.shape
    depth = w_in.shape[0]
    rows = b * seq
    d_cp = D_CONV + D_POOL
    row = lambda v: v.reshape(1, -1).astype(F32)
    xf = x.reshape(rows, d).astype(F32)
    for l in range(depth):
        lambda_init = 0.8 - 0.6 * math.exp(-0.3 * l)
        x1, ucp, uq, uk, uv = _ffn1_win(
            xf, row(ffn1_norm[l]), ffn1_w_gate[l].astype(BF16), ffn1_w_up[l].astype(BF16),
            ffn1_w_down[l].astype(BF16), row(mix_norm[l]), w_in[l].astype(BF16))
        ycp = _convpool(
            ucp.reshape(b, seq, -1), conv_dw[l].astype(F32), row(conv_dw_bias[l]),
            row(conv_ln_gain[l]), row(conv_ln_bias[l]),
            _pool_block_diag(pool_w[l]).astype(BF16), row(pool_scale[l]))
        gq2 = jnp.concatenate([q_norm[l], q_norm[l]]).reshape(1, -1).astype(F32)
        gk2 = jnp.concatenate([k_norm[l], k_norm[l]]).reshape(1, -1).astype(F32)
        qT, ka, vT = _qkv_prep(uq.reshape(b, seq, -1), uk.reshape(b, seq, -1),
                               uv.reshape(b, seq, -1), gq2, gk2)
        yat = _attention(qT, ka, vT, row(lambda_q1[l]), row(lambda_k1[l]), row(lambda_q2[l]),
                         row(lambda_k2[l]), attn_subln[l].reshape(-1, 1).astype(F32), lambda_init)
        wo = w_out[l].astype(BF16)
        xf = _out_ffn2(x1, ycp.reshape(rows, d_cp), yat.reshape(rows, -1), wo[:d_cp], wo[d_cp:],
                       row(ffn2_norm[l]), ffn2_w_gate[l].astype(BF16), ffn2_w_up[l].astype(BF16),
                       ffn2_w_down[l].astype(BF16), row(post_norm[l]))
    return xf.reshape(b, seq, d).astype(x.dtype)
```

```python
import functools
import math

import jax
import jax.numpy as jnp
from jax import lax
from jax.experimental import pallas as pl
from jax.experimental.pallas import tpu as pltpu

F32 = jnp.float32
BF16 = jnp.bfloat16

NORM_EPS = 1e-6
N_HEADS = 4
HEAD_DIM = 64
V_DIM = 2 * HEAD_DIM
D_CONV = 256
D_POOL = 256
D_ATTN = 512
CONV_WIDTH = 31
POOL_GROUP_DIM = 64
HALO = 16
V_ROWS = V_DIM + 8
AUG = 4
BF16_EXACT_INT = 256
VMEM_LIMIT_BYTES = 56 * 1024 * 1024
NEG_BIG = -1e30
MAX_UNSHIFTED_SCORE = 60.0
UNSHIFTED_CHUNK = 4


def _attn_block(seq):
    return min(512, seq)


def _row_block(rows):
    return min(256, rows)


def _rms(x, g):
    ms = jnp.mean(x * x, axis=-1, keepdims=True)
    return x * lax.rsqrt(ms + NORM_EPS) * g


def _sigmoid(x):
    return 1.0 / (1.0 + jnp.exp(-x))


def _swiglu(h, wg_ref, wu_ref, wd_ref):
    hb = h.astype(BF16)
    g = jnp.dot(hb, wg_ref[...], preferred_element_type=F32)
    u = jnp.dot(hb, wu_ref[...], preferred_element_type=F32)
    a = (g * _sigmoid(g) * u).astype(BF16)
    return jnp.dot(a, wd_ref[...], preferred_element_type=F32)


def _resident(shape):
    nd = len(shape)
    return pl.BlockSpec(shape, lambda *_: (0,) * nd, pipeline_mode=pl.Buffered(1))


def _ffn1_win_kernel(x_ref, g1_ref, wg_ref, wu_ref, wd_ref, gm_ref, win_ref,
                     x1_ref, ucp_ref, uq_ref, uk_ref, uv_ref):
    x = x_ref[...]
    x1 = x + 0.5 * _swiglu(_rms(x, g1_ref[...]), wg_ref, wu_ref, wd_ref)
    x1_ref[...] = x1
    hm = _rms(x1, gm_ref[...]).astype(BF16)
    u = jnp.dot(hm, win_ref[...], preferred_element_type=F32)
    c0 = 2 * D_CONV + D_POOL
    ucp_ref[...] = u[:, :c0]
    uq_ref[...] = u[:, c0:c0 + D_ATTN]
    uk_ref[...] = u[:, c0 + D_ATTN:c0 + 2 * D_ATTN]
    uv_ref[...] = u[:, c0 + 2 * D_ATTN:]


def _ffn1_win(x, g1, wg, wu, wd, gm, win):
    rows, d = x.shape
    f = wg.shape[1]
    tm = _row_block(rows)
    c0 = 2 * D_CONV + D_POOL
    row_spec = lambda w: pl.BlockSpec((tm, w), lambda i: (i, 0))
    return pl.pallas_call(
        _ffn1_win_kernel,
        grid=(rows // tm,),
        in_specs=[row_spec(d), _resident((1, d)), _resident((d, f)), _resident((d, f)),
                  _resident((f, d)), _resident((1, d)), _resident(win.shape)],
        out_specs=[row_spec(d), row_spec(c0), row_spec(D_ATTN), row_spec(D_ATTN), row_spec(D_ATTN)],
        out_shape=[jax.ShapeDtypeStruct((rows, d), F32),
                   jax.ShapeDtypeStruct((rows, c0), F32),
                   jax.ShapeDtypeStruct((rows, D_ATTN), F32),
                   jax.ShapeDtypeStruct((rows, D_ATTN), F32),
                   jax.ShapeDtypeStruct((rows, D_ATTN), F32)],
        compiler_params=pltpu.CompilerParams(dimension_semantics=("arbitrary",),
                                             vmem_limit_bytes=VMEM_LIMIT_BYTES),
        name="ffn1_win",
    )(x, g1, wg, wu, wd, gm, win)


def _convpool_kernel(seq, main_ref, prev_ref, next_ref, dw_ref, dwb_ref, lng_ref, lnb_ref,
                     pw_ref, ps_ref, out_ref, zext, pext, a1, a2, a3):
    ts = main_ref.shape[0]
    i = pl.program_id(1)
    has_prev = i > 0
    has_next = i < pl.num_programs(1) - 1

    def glu(u):
        return u[:, :D_CONV] * _sigmoid(u[:, D_CONV:2 * D_CONV])

    um = main_ref[...]
    up = prev_ref[...]
    un = next_ref[...]
    zero_halo = jnp.zeros((HALO, D_CONV), F32)
    zext[0:HALO, :] = jnp.where(has_prev, glu(up), zero_halo)
    zext[HALO:HALO + ts, :] = glu(um)
    zext[HALO + ts:, :] = jnp.where(has_next, glu(un), zero_halo)
    pext[0:HALO, :] = jnp.where(has_prev, up[:, 2 * D_CONV:], zero_halo)
    pext[HALO:HALO + ts, :] = um[:, 2 * D_CONV:]
    pext[HALO + ts:, :] = jnp.where(has_next, un[:, 2 * D_CONV:], zero_halo)

    off0 = HALO - CONV_WIDTH // 2
    acc = jnp.zeros((ts, D_CONV), F32)
    for j in range(CONV_WIDTH):
        acc = acc + dw_ref[j:j + 1, :] * zext[off0 + j:off0 + j + ts, :]
    z = acc + dwb_ref[...]
    mu = jnp.mean(z, axis=-1, keepdims=True)
    zc = z - mu
    var = jnp.mean(zc * zc, axis=-1, keepdims=True)
    y = zc * lax.rsqrt(var + NORM_EPS) * lng_ref[...] + lnb_ref[...]
    out_ref[:, :D_CONV] = (y * _sigmoid(y)).astype(out_ref.dtype)

    n = ts + 2 * HALO
    a1[1:n, :] = pext[0:n - 1, :] + pext[1:n, :]
    a2[2:n - 1, :] = a1[1:n - 2, :] + a1[3:n, :]
    a3[4:n - 3, :] = a2[2:n - 5, :] + a2[6:n - 1, :]
    w2 = a1[HALO:HALO + ts, :]
    w4 = a2[HALO:HALO + ts, :]
    w8 = a3[HALO:HALO + ts, :]
    w16 = a3[HALO - 4:HALO - 4 + ts, :] + a3[HALO + 4:HALO + 4 + ts, :]
    grp = lax.broadcasted_iota(jnp.int32, (ts, D_POOL), 1) // POOL_GROUP_DIM
    win = jnp.where(grp == 0, w2, jnp.where(grp == 1, w4, jnp.where(grp == 2, w8, w16)))
    t = lax.broadcasted_iota(jnp.int32, (ts, D_POOL), 0) + i * ts
    half = jnp.left_shift(1, grp)
    cnt = jnp.minimum(t + half, seq) - jnp.maximum(t - half, 0)
    d = win / cnt.astype(F32) - pext[HALO:HALO + ts, :]
    yp = jnp.dot(d.astype(BF16), pw_ref[...], preferred_element_type=F32) * ps_ref[...]
    out_ref[:, D_CONV:] = yp.astype(out_ref.dtype)


def _convpool(ucp, dw, dwb, lng, lnb, pw_bd, ps):
    b, seq, c0 = ucp.shape
    ts = _attn_block(seq)
    nh = ts // HALO
    last_halo = seq // HALO - 1
    n = ts + 2 * HALO
    return pl.pallas_call(
        functools.partial(_convpool_kernel, seq),
        grid=(b, seq // ts),
        in_specs=[pl.BlockSpec((None, ts, c0), lambda bi, i: (bi, i, 0)),
                  pl.BlockSpec((None, HALO, c0), lambda bi, i: (bi, jnp.maximum(i * nh - 1, 0), 0)),
                  pl.BlockSpec((None, HALO, c0),
                               lambda bi, i: (bi, jnp.minimum((i + 1) * nh, last_halo), 0)),
                  _resident(dw.shape), _resident(dwb.shape), _resident(lng.shape),
                  _resident(lnb.shape), _resident(pw_bd.shape), _resident(ps.shape)],
        out_specs=pl.BlockSpec((None, ts, D_CONV + D_POOL), lambda bi, i: (bi, i, 0)),
        out_shape=jax.ShapeDtypeStruct((b, seq, D_CONV + D_POOL), BF16),
        scratch_shapes=[pltpu.VMEM((n, D_CONV), F32), pltpu.VMEM((n, D_POOL), F32),
                        pltpu.VMEM((n, D_POOL), F32), pltpu.VMEM((n, D_POOL), F32),
                        pltpu.VMEM((n, D_POOL), F32)],
        compiler_params=pltpu.CompilerParams(dimension_semantics=("arbitrary", "arbitrary"),
                                             vmem_limit_bytes=VMEM_LIMIT_BYTES),
        name="convpool",
    )(ucp, ucp, ucp, dw, dwb, lng, lnb, pw_bd, ps)


def _slope(h):
    return 2.0 ** (-8.0 * (h + 1) / N_HEADS)


def _qkv_prep_kernel(blk, uq_ref, uk_ref, uv_ref, gq_ref, gk_ref, qT_ref, ka_ref, vT_ref):
    ts = uq_ref.shape[0]
    i = pl.program_id(1)
    lane = lax.broadcasted_iota(jnp.int32, (ts, V_DIM), 1)
    low = lane < HEAD_DIM
    pos = (lax.broadcasted_iota(jnp.int32, (ts, V_DIM), 0) + i * ts) % blk
    pos_lo = (pos % BF16_EXACT_INT).astype(F32)
    pos_hi = (pos - pos % BF16_EXACT_INT).astype(F32)
    one = jnp.ones((ts, V_DIM), F32)
    zero = jnp.zeros((ts, V_DIM), F32)
    row8 = lax.broadcasted_iota(jnp.int32, (V_ROWS - V_DIM, ts), 0)
    ones_rows = jnp.where(row8 == 0, 1.0, 0.0).astype(vT_ref.dtype)

    def norm_pair(x, g):
        x2 = x * x
        s_lo = jnp.sum(jnp.where(low, x2, 0.0), axis=-1, keepdims=True)
        s_hi = jnp.sum(jnp.where(low, 0.0, x2), axis=-1, keepdims=True)
        ms = jnp.where(low, s_lo, s_hi) * (1.0 / HEAD_DIM)
        return x * lax.rsqrt(ms + NORM_EPS) * g

    def aug(c0, c1, c2, c3):
        a = jnp.where(lane == HEAD_DIM, c0, zero)
        a = jnp.where(lane == HEAD_DIM + 1, c1, a)
        a = jnp.where(lane == HEAD_DIM + 2, c2, a)
        return jnp.where(lane == HEAD_DIM + 3, c3, a)

    for h in range(N_HEADS):
        sl = _slope(h)
        cols = slice(h * V_DIM, (h + 1) * V_DIM)
        qn = norm_pair(uq_ref[:, cols], gq_ref[...]) * (HEAD_DIM ** -0.5)
        kn = norm_pair(uk_ref[:, cols], gk_ref[...])
        q_aug = aug(one, one, sl * pos_lo, sl * pos_hi)
        k_aug = aug(-sl * pos_lo, -sl * pos_hi, one, one)
        for c in range(2):
            qc = qn if c == 0 else pltpu.roll(qn, HEAD_DIM, 1)
            kc = kn if c == 0 else pltpu.roll(kn, HEAD_DIM, 1)
            qT_ref[h, c] = jnp.where(low, qc, q_aug).T.astype(qT_ref.dtype)
            ka_ref[h, c] = jnp.where(low, kc, k_aug).astype(ka_ref.dtype)
        vT_ref[h, 0, :V_DIM, :] = uv_ref[:, cols].T.astype(vT_ref.dtype)
        vT_ref[h, 0, V_DIM:, :] = ones_rows


def _qkv_prep(uq, uk, uv, gq, gk):
    b, seq, _ = uq.shape
    blk = _attn_block(seq)
    nblk = seq // blk
    in_spec = pl.BlockSpec((None, blk, D_ATTN), lambda bi, i: (bi, i, 0))
    return pl.pallas_call(
        functools.partial(_qkv_prep_kernel, blk),
        grid=(b, nblk),
        in_specs=[in_spec, in_spec, in_spec, _resident(gq.shape), _resident(gk.shape)],
        out_specs=[pl.BlockSpec((None, N_HEADS, 2, V_DIM, blk), lambda bi, i: (bi, 0, 0, 0, i)),
                   pl.BlockSpec((None, N_HEADS, 2, blk, V_DIM), lambda bi, i: (bi, 0, 0, i, 0)),
                   pl.BlockSpec((None, N_HEADS, 1, V_ROWS, blk), lambda bi, i: (bi, 0, i, 0, 0))],
        out_shape=[jax.ShapeDtypeStruct((b, N_HEADS, 2, V_DIM, seq), BF16),
                   jax.ShapeDtypeStruct((b, N_HEADS, 2, seq, V_DIM), BF16),
                   jax.ShapeDtypeStruct((b, N_HEADS, nblk, V_ROWS, blk), BF16)],
        compiler_params=pltpu.CompilerParams(dimension_semantics=("arbitrary", "arbitrary"),
                                             vmem_limit_bytes=VMEM_LIMIT_BYTES),
        name="qkv_prep",
    )(uq, uk, uv, gq, gk)


def _attn_kernel(lambda_init, slopes_ref, unshifted_ref, qT_ref, ka_ref, vT_ref, lq1_ref, lk1_ref,
                 lq2_ref, lk2_ref, sub_ref, out_ref, qv_ref, m_ref, acc_ref):
    h = pl.program_id(1)
    i = pl.program_id(2)
    nk = ka_ref.shape[1]
    tk = ka_ref.shape[2]
    tq = qT_ref.shape[2]
    slope = slopes_ref[h]

    row = lax.broadcasted_iota(jnp.int32, (V_DIM, tq), 0)
    for c in range(2):
        q = qT_ref[c]
        qv_ref[0, c] = jnp.where(row < HEAD_DIM, q, -q)
        qv_ref[1, c] = jnp.where(row < HEAD_DIM, q, jnp.zeros_like(q))
        qv_ref[2, c] = q
    acc_ref[...] = jnp.zeros(acc_ref.shape, F32)

    def diag_unit():
        kk = lax.broadcasted_iota(jnp.int32, (tk, tq), 0)
        qq = lax.broadcasted_iota(jnp.int32, (tk, tq), 1)
        return i, 1, slope * jnp.abs(kk - qq).astype(F32)

    def off_diag_unit(t):
        j = jnp.where(t >= i, t + 1, t)
        dist = jnp.abs(j - i) * tk
        return j, jnp.where(j > i, 2, 0), slope * jnp.full((1, tq), dist, jnp.int32).astype(F32)

    def scores(unit, c):
        j, side, bias = unit
        return jnp.dot(ka_ref[c, j], qv_ref[side, c], preferred_element_type=F32) - bias

    def unshifted_chunk(units):
        chains = [(u, c) for u in units for c in range(2)]
        probs = [None] * len(chains)
        pv = [[], []]

        def score_stage(k):
            probs[k] = jnp.exp(scores(*chains[k])).astype(BF16)

        def pv_stage(k):
            (j, _, _), c = chains[k]
            pv[c].append(jnp.dot(vT_ref[j], probs[k], preferred_element_type=F32))

        score_stage(0)
        for k in range(1, len(chains)):
            score_stage(k)
            pv_stage(k - 1)
        pv_stage(len(chains) - 1)
        for c in range(2):
            acc_ref[c] += functools.reduce(lambda a, b: a + b, pv[c])

    def running_max_step(unit):
        for c in range(2):
            s = scores(unit, c)
            m_old = m_ref[c]
            m_new = jnp.maximum(m_old, jnp.max(s, axis=0, keepdims=True))
            p = jnp.exp(s - m_new).astype(BF16)
            alpha = jnp.exp(m_old - m_new)
            acc_ref[c] = acc_ref[c] * alpha + jnp.dot(vT_ref[unit[0]], p, preferred_element_type=F32)
            m_ref[c] = m_new

    chunk = min(UNSHIFTED_CHUNK, nk)

    @pl.when(unshifted_ref[0] != 0)
    def _():
        unshifted_chunk([diag_unit()] + [off_diag_unit(t) for t in range(chunk - 1)])

        def body(g, carry):
            t0 = chunk - 1 + g * chunk
            unshifted_chunk([off_diag_unit(t0 + r) for r in range(chunk)])
            return carry

        lax.fori_loop(0, nk // chunk - 1, body, 0)

    @pl.when(unshifted_ref[0] == 0)
    def _():
        m_ref[...] = jnp.full(m_ref.shape, NEG_BIG, F32)
        running_max_step(diag_unit())

        def body(t, carry):
            running_max_step(off_diag_unit(t))
            return carry

        lax.fori_loop(0, nk - 1, body, 0)

    lam = (jnp.exp(jnp.sum(lq1_ref[...] * lk1_ref[...], axis=-1, keepdims=True))
           - jnp.exp(jnp.sum(lq2_ref[...] * lk2_ref[...], axis=-1, keepdims=True))
           + lambda_init)
    a0 = acc_ref[0]
    a1 = acc_ref[1]
    o = a0[:V_DIM] / a0[V_DIM:V_DIM + 1] - lam * (a1[:V_DIM] / a1[V_DIM:V_DIM + 1])
    ms = jnp.mean(o * o, axis=0, keepdims=True)
    y = o * lax.rsqrt(ms + NORM_EPS) * sub_ref[...] * (1.0 - lambda_init)
    out_ref[...] = y.T.astype(out_ref.dtype)


def _attention(qT, ka, vT, score_bound, lq1, lk1, lq2, lk2, sub_col, lambda_init):
    b, nh, _, _, seq = qT.shape
    blk = vT.shape[-1]
    nblk = seq // blk
    assert nblk % min(UNSHIFTED_CHUNK, nblk) == 0
    ka = ka.reshape(b, nh, 2, nblk, blk, V_DIM)
    slopes = jnp.asarray([_slope(h) for h in range(nh)], F32)
    unshifted = (score_bound <= MAX_UNSHIFTED_SCORE).astype(jnp.int32).reshape(1)
    return pl.pallas_call(
        functools.partial(_attn_kernel, lambda_init),
        grid=(b, nh, nblk),
        in_specs=[pl.BlockSpec(memory_space=pltpu.SMEM),
                  pl.BlockSpec(memory_space=pltpu.SMEM),
                  pl.BlockSpec((None, None, 2, V_DIM, blk), lambda bi, h, i: (bi, h, 0, 0, i)),
                  pl.BlockSpec((None, None, 2, nblk, blk, V_DIM), lambda bi, h, i: (bi, h, 0, 0, 0, 0)),
                  pl.BlockSpec((None, None, nblk, V_ROWS, blk), lambda bi, h, i: (bi, h, 0, 0, 0)),
                  _resident(lq1.shape), _resident(lk1.shape), _resident(lq2.shape),
                  _resident(lk2.shape), _resident(sub_col.shape)],
        out_specs=pl.BlockSpec((None, blk, V_DIM), lambda bi, h, i: (bi, i, h)),
        out_shape=jax.ShapeDtypeStruct((b, seq, nh * V_DIM), BF16),
        scratch_shapes=[pltpu.VMEM((3, 2, V_DIM, blk), BF16),
                        pltpu.VMEM((2, 1, blk), F32),
                        pltpu.VMEM((2, V_ROWS, blk), F32)],
        compiler_params=pltpu.CompilerParams(
            dimension_semantics=("arbitrary", "arbitrary", "arbitrary"),
            vmem_limit_bytes=VMEM_LIMIT_BYTES),
        name="attention",
    )(slopes, unshifted, qT, ka, vT, lq1, lk1, lq2, lk2, sub_col)


def _out_ffn2_kernel(x1_ref, ycp_ref, yat_ref, wo_cp_ref, wo_at_ref, g2_ref, wg_ref, wu_ref,
                     wd_ref, gp_ref, out_ref):
    x2 = (x1_ref[...]
          + jnp.dot(ycp_ref[...], wo_cp_ref[...], preferred_element_type=F32)
          + jnp.dot(yat_ref[...], wo_at_ref[...], preferred_element_type=F32))
    x3 = x2 + 0.5 * _swiglu(_rms(x2, g2_ref[...]), wg_ref, wu_ref, wd_ref)
    out_ref[...] = _rms(x3, gp_ref[...])


def _out_ffn2(x1, ycp, yat, wo_cp, wo_at, g2, wg, wu, wd, gp):
    rows, d = x1.shape
    f = wg.shape[1]
    tm = _row_block(rows)
    row_spec = lambda w: pl.BlockSpec((tm, w), lambda i: (i, 0))
    return pl.pallas_call(
        _out_ffn2_kernel,
        grid=(rows // tm,),
        in_specs=[row_spec(d), row_spec(ycp.shape[1]), row_spec(yat.shape[1]),
                  _resident(wo_cp.shape), _resident(wo_at.shape), _resident((1, d)),
                  _resident((d, f)), _resident((d, f)), _resident((f, d)), _resident((1, d))],
        out_specs=row_spec(d),
        out_shape=jax.ShapeDtypeStruct((rows, d), F32),
        compiler_params=pltpu.CompilerParams(dimension_semantics=("arbitrary",),
                                             vmem_limit_bytes=VMEM_LIMIT_BYTES),
        name="out_ffn2",
    )(x1, ycp, yat, wo_cp, wo_at, g2, wg, wu, wd, gp)


def _pool_block_diag(pool_w):
    g, cg, _ = pool_w.shape
    out = jnp.zeros((g * cg, g * cg), pool_w.dtype)
    for k in range(g):
        out = out.at[k * cg:(k + 1) * cg, k * cg:(k + 1) * cg].set(pool_w[k])
    return out


def kernel(x, ffn1_norm, ffn1_w_gate, ffn1_w_up, ffn1_w_down, mix_norm, w_in, conv_dw, conv_dw_bias, conv_ln_gain, conv_ln_bias, pool_w, pool_scale, q_norm, k_norm, lambda_q1, lambda_k1, lambda_q2, lambda_k2, attn_subln, w_out, ffn2_norm, ffn2_w_gate, ffn2_w_up, ffn2_w_down, post_norm):
    b, seq, d = x.shape
    depth = w_in.shape[0]
    rows = b * seq
    d_cp = D_CONV + D_POOL
    row = lambda v: v.reshape(1, -1).astype(F32)
    xf = x.reshape(rows, d).astype(F32)
    for l in range(depth):
        lambda_init = 0.8 - 0.6 * math.exp(-0.3 * l)
        x1, ucp, uq, uk, uv = _ffn1_win(
            xf, row(ffn1_norm[l]), ffn1_w_gate[l].astype(BF16), ffn1_w_up[l].astype(BF16),
            ffn1_w_down[l].astype(BF16), row(mix_norm[l]), w_in[l].astype(BF16))
        ycp = _convpool(
            ucp.reshape(b, seq, -1), conv_dw[l].astype(F32), row(conv_dw_bias[l]),
            row(conv_ln_gain[l]), row(conv_ln_bias[l]),
            _pool_block_diag(pool_w[l]).astype(BF16), row(pool_scale[l]))
        gq2 = jnp.concatenate([q_norm[l], q_norm[l]]).reshape(1, -1).astype(F32)
        gk2 = jnp.concatenate([k_norm[l], k_norm[l]]).reshape(1, -1).astype(F32)
        qT, ka, vT = _qkv_prep(uq.reshape(b, seq, -1), uk.reshape(b, seq, -1),
                               uv.reshape(b, seq, -1), gq2, gk2)
        score_bound = (1.01 * HEAD_DIM ** 0.5) * jnp.max(jnp.abs(q_norm[l])) * jnp.max(jnp.abs(k_norm[l]))
        yat = _attention(qT, ka, vT, score_bound, row(lambda_q1[l]), row(lambda_k1[l]), row(lambda_q2[l]),
                         row(lambda_k2[l]), attn_subln[l].reshape(-1, 1).astype(F32), lambda_init)
        wo = w_out[l].astype(BF16)
        xf = _out_ffn2(x1, ycp.reshape(rows, d_cp), yat.reshape(rows, -1), wo[:d_cp], wo[d_cp:],
                       row(ffn2_norm[l]), ffn2_w_gate[l].astype(BF16), ffn2_w_up[l].astype(BF16),
                       ffn2_w_down[l].astype(BF16), row(post_norm[l]))
    return xf.reshape(b, seq, d).astype(x.dtype)
```

```python
import functools
import math

import jax
import jax.numpy as jnp
from jax import lax
from jax.experimental import pallas as pl
from jax.experimental.pallas import tpu as pltpu

F32 = jnp.float32
BF16 = jnp.bfloat16
F8 = jnp.float8_e4m3fn

NORM_EPS = 1e-6
N_HEADS = 4
HEAD_DIM = 64
V_DIM = 2 * HEAD_DIM
D_CONV = 256
D_POOL = 256
D_ATTN = 512
CONV_WIDTH = 31
POOL_GROUP_DIM = 64
SUBLANES = 8
HALO = 16
V_ROWS = V_DIM + 8
BF16_EXACT_INT = 256
VMEM_LIMIT_BYTES = 56 * 1024 * 1024
NEG_BIG = -1e30
MAX_UNSHIFTED_SCORE = 60.0
UNSHIFTED_CHUNK = 8
QK_STACK = 4 * HEAD_DIM
F8_EXACT_INT = 16
F8_TARGET_MAX = 256.0
SCORE_SCALE = 512.0
LOG2_E = 1.4426950408889634


def _attn_block(seq):
    return min(512, seq)


def _row_block(rows):
    return min(256, rows)


def _rms(x, g):
    ms = jnp.mean(x * x, axis=-1, keepdims=True)
    return x * lax.rsqrt(ms + NORM_EPS) * g


def _sigmoid(x):
    return 1.0 / (1.0 + jnp.exp(-x))


def _swiglu(h, wg_ref, wu_ref, wd_ref):
    hb = h.astype(BF16)
    g = jnp.dot(hb, wg_ref[...], preferred_element_type=F32)
    u = jnp.dot(hb, wu_ref[...], preferred_element_type=F32)
    a = (g * _sigmoid(g) * u).astype(BF16)
    return jnp.dot(a, wd_ref[...], preferred_element_type=F32)


def _resident(shape):
    nd = len(shape)
    return pl.BlockSpec(shape, lambda *_: (0,) * nd, pipeline_mode=pl.Buffered(1))


def _ffn1_win_kernel(x_ref, g1_ref, wg_ref, wu_ref, wd_ref, gm_ref, win_ref,
                     x1_ref, ucp_ref, uq_ref, uk_ref, uv_ref):
    x = x_ref[...]
    x1 = x + 0.5 * _swiglu(_rms(x, g1_ref[...]), wg_ref, wu_ref, wd_ref)
    x1_ref[...] = x1
    hm = _rms(x1, gm_ref[...]).astype(BF16)
    u = jnp.dot(hm, win_ref[...], preferred_element_type=F32)
    c0 = 2 * D_CONV + D_POOL
    ucp_ref[...] = u[:, :c0]
    uq_ref[...] = u[:, c0:c0 + D_ATTN]
    uk_ref[...] = u[:, c0 + D_ATTN:c0 + 2 * D_ATTN]
    uv_ref[...] = u[:, c0 + 2 * D_ATTN:]


def _ffn1_win(x, g1, wg, wu, wd, gm, win):
    rows, d = x.shape
    f = wg.shape[1]
    tm = _row_block(rows)
    c0 = 2 * D_CONV + D_POOL
    row_spec = lambda w: pl.BlockSpec((tm, w), lambda i: (i, 0))
    return pl.pallas_call(
        _ffn1_win_kernel,
        grid=(rows // tm,),
        in_specs=[row_spec(d), _resident((1, d)), _resident((d, f)), _resident((d, f)),
                  _resident((f, d)), _resident((1, d)), _resident(win.shape)],
        out_specs=[row_spec(d), row_spec(c0), row_spec(D_ATTN), row_spec(D_ATTN), row_spec(D_ATTN)],
        out_shape=[jax.ShapeDtypeStruct((rows, d), F32),
                   jax.ShapeDtypeStruct((rows, c0), F32),
                   jax.ShapeDtypeStruct((rows, D_ATTN), F32),
                   jax.ShapeDtypeStruct((rows, D_ATTN), F32),
                   jax.ShapeDtypeStruct((rows, D_ATTN), F32)],
        compiler_params=pltpu.CompilerParams(dimension_semantics=("arbitrary",),
                                             vmem_limit_bytes=VMEM_LIMIT_BYTES),
        name="ffn1_win",
    )(x, g1, wg, wu, wd, gm, win)


def _convpool_kernel(seq, main_ref, prev_ref, next_ref, dw_ref, dwb_ref, lng_ref, lnb_ref,
                     pw_ref, ps_ref, out_ref, zext, zrot, pext, a1, a2, a3):
    ts = main_ref.shape[0]
    i = pl.program_id(1)
    has_prev = i > 0
    has_next = i < pl.num_programs(1) - 1

    def glu(u):
        return u[:, :D_CONV] * _sigmoid(u[:, D_CONV:2 * D_CONV])

    um = main_ref[...]
    up = prev_ref[...]
    un = next_ref[...]
    zero_halo = jnp.zeros((HALO, D_CONV), F32)
    zext[0:HALO, :] = jnp.where(has_prev, glu(up), zero_halo)
    zext[HALO:HALO + ts, :] = glu(um)
    zext[HALO + ts:, :] = jnp.where(has_next, glu(un), zero_halo)
    pext[0:HALO, :] = jnp.where(has_prev, up[:, 2 * D_CONV:], zero_halo)
    pext[HALO:HALO + ts, :] = um[:, 2 * D_CONV:]
    pext[HALO + ts:, :] = jnp.where(has_next, un[:, 2 * D_CONV:], zero_halo)

    off0 = HALO - CONV_WIDTH // 2
    nz = ts + 2 * HALO - SUBLANES
    for r in range(1, SUBLANES):
        zrot[r - 1, 0:nz, :] = zext[r:r + nz, :]
    acc = jnp.zeros((ts, D_CONV), F32)
    for j in range(CONV_WIDTH):
        r, a = (off0 + j) % SUBLANES, (off0 + j) // SUBLANES * SUBLANES
        tap = zext[a:a + ts, :] if r == 0 else zrot[r - 1, a:a + ts, :]
        acc = acc + dw_ref[j:j + 1, :] * tap
    z = acc + dwb_ref[...]
    mu = jnp.mean(z, axis=-1, keepdims=True)
    zc = z - mu
    var = jnp.mean(zc * zc, axis=-1, keepdims=True)
    y = zc * lax.rsqrt(var + NORM_EPS) * lng_ref[...] + lnb_ref[...]
    out_ref[:, :D_CONV] = (y * _sigmoid(y)).astype(out_ref.dtype)

    n = ts + 2 * HALO
    a1[1:n, :] = pext[0:n - 1, :] + pext[1:n, :]
    a2[2:n - 1, :] = a1[1:n - 2, :] + a1[3:n, :]
    a3[4:n - 3, :] = a2[2:n - 5, :] + a2[6:n - 1, :]
    w2 = a1[HALO:HALO + ts, :]
    w4 = a2[HALO:HALO + ts, :]
    w8 = a3[HALO:HALO + ts, :]
    w16 = a3[HALO - 4:HALO - 4 + ts, :] + a3[HALO + 4:HALO + 4 + ts, :]
    grp = lax.broadcasted_iota(jnp.int32, (ts, D_POOL), 1) // POOL_GROUP_DIM
    win = jnp.where(grp == 0, w2, jnp.where(grp == 1, w4, jnp.where(grp == 2, w8, w16)))
    t = lax.broadcasted_iota(jnp.int32, (ts, D_POOL), 0) + i * ts
    half = jnp.left_shift(1, grp)
    cnt = jnp.minimum(t + half, seq) - jnp.maximum(t - half, 0)
    d = win / cnt.astype(F32) - pext[HALO:HALO + ts, :]
    yp = jnp.dot(d.astype(BF16), pw_ref[...], preferred_element_type=F32) * ps_ref[...]
    out_ref[:, D_CONV:] = yp.astype(out_ref.dtype)


def _convpool(ucp, dw, dwb, lng, lnb, pw_bd, ps):
    b, seq, c0 = ucp.shape
    ts = _attn_block(seq)
    nh = ts // HALO
    last_halo = seq // HALO - 1
    n = ts + 2 * HALO
    return pl.pallas_call(
        functools.partial(_convpool_kernel, seq),
        grid=(b, seq // ts),
        in_specs=[pl.BlockSpec((None, ts, c0), lambda bi, i: (bi, i, 0)),
                  pl.BlockSpec((None, HALO, c0), lambda bi, i: (bi, jnp.maximum(i * nh - 1, 0), 0)),
                  pl.BlockSpec((None, HALO, c0),
                               lambda bi, i: (bi, jnp.minimum((i + 1) * nh, last_halo), 0)),
                  _resident(dw.shape), _resident(dwb.shape), _resident(lng.shape),
                  _resident(lnb.shape), _resident(pw_bd.shape), _resident(ps.shape)],
        out_specs=pl.BlockSpec((None, ts, D_CONV + D_POOL), lambda bi, i: (bi, i, 0)),
        out_shape=jax.ShapeDtypeStruct((b, seq, D_CONV + D_POOL), BF16),
        scratch_shapes=[pltpu.VMEM((n, D_CONV), F32), pltpu.VMEM((SUBLANES - 1, n, D_CONV), F32),
                        pltpu.VMEM((n, D_POOL), F32),
                        pltpu.VMEM((n, D_POOL), F32), pltpu.VMEM((n, D_POOL), F32),
                        pltpu.VMEM((n, D_POOL), F32)],
        compiler_params=pltpu.CompilerParams(dimension_semantics=("arbitrary", "arbitrary"),
                                             vmem_limit_bytes=VMEM_LIMIT_BYTES),
        name="convpool",
    )(ucp, ucp, ucp, dw, dwb, lng, lnb, pw_bd, ps)


def _slope(h):
    return 2.0 ** (-8.0 * (h + 1) / N_HEADS)


def _qkv_prep_kernel(blk, stacked, scale_ref, uq_ref, uk_ref, uv_ref, gq_ref, gk_ref,
                     qT_ref, ka_ref, vT_ref):
    ts = uq_ref.shape[0]
    i = pl.program_id(1)
    lane = lax.broadcasted_iota(jnp.int32, (ts, V_DIM), 1)
    low = lane < HEAD_DIM
    pos = (lax.broadcasted_iota(jnp.int32, (ts, V_DIM), 0) + i * ts) % blk
    one = jnp.ones((ts, V_DIM), F32)
    zero = jnp.zeros((ts, V_DIM), F32)
    row8 = lax.broadcasted_iota(jnp.int32, (V_ROWS - V_DIM, ts), 0)
    ones_rows = jnp.where(row8 == 0, 1.0, 0.0).astype(vT_ref.dtype)
    exact = F8_EXACT_INT if stacked else BF16_EXACT_INT
    digits = [(pos % exact).astype(F32)]
    place = exact
    while place < blk:
        digits.append((pos % (place * exact) - pos % place).astype(F32))
        place *= exact

    def norm_pair(x, g):
        x2 = x * x
        s_lo = jnp.sum(jnp.where(low, x2, 0.0), axis=-1, keepdims=True)
        s_hi = jnp.sum(jnp.where(low, 0.0, x2), axis=-1, keepdims=True)
        ms = jnp.where(low, s_lo, s_hi) * (1.0 / HEAD_DIM)
        return x * lax.rsqrt(ms + NORM_EPS) * g

    def aug(cols):
        a = zero
        for n, col in enumerate(cols):
            a = jnp.where(lane == HEAD_DIM + n, col, a)
        return a

    def other_map(x):
        return pltpu.roll(x, HEAD_DIM, 1)

    def rounded(x):
        return x.astype(F8).astype(F32)

    for h in range(N_HEADS):
        cols = slice(h * V_DIM, (h + 1) * V_DIM)
        qn = norm_pair(uq_ref[:, cols], gq_ref[...]) * (HEAD_DIM ** -0.5)
        kn = norm_pair(uk_ref[:, cols], gk_ref[...])
        coef = _slope(h) * (SCORE_SCALE if stacked else 1.0)
        q_aug = aug([-coef * one] * len(digits) + digits)
        k_aug = aug(digits + [coef * one] * len(digits))
        if stacked:
            qs = qn * scale_ref[0]
            ks = kn * scale_ref[1]
            qh, kh = rounded(qs), rounded(ks)
            ql, kl = rounded(qs - qh), rounded(ks - kh)
            for c in range(2):
                qh_c = qh if c == 0 else other_map(qh)
                kh_c = kh if c == 0 else other_map(kh)
                q_first = jnp.where(low, qh_c, other_map(ql) if c == 0 else ql)
                k_first = jnp.where(low, kh_c, other_map(kh) if c == 0 else kh)
                q_second = jnp.where(low, qh_c, q_aug)
                k_second = jnp.where(low, kl if c == 0 else other_map(kl), k_aug)
                qT_ref[h, c, :V_DIM, :] = q_first.T.astype(qT_ref.dtype)
                qT_ref[h, c, V_DIM:, :] = q_second.T.astype(qT_ref.dtype)
                ka_ref[h, c, :, :V_DIM] = k_first.astype(ka_ref.dtype)
                ka_ref[h, c, :, V_DIM:] = k_second.astype(ka_ref.dtype)
        else:
            for c in range(2):
                qc = qn if c == 0 else other_map(qn)
                kc = kn if c == 0 else other_map(kn)
                qT_ref[h, c] = jnp.where(low, qc, q_aug).T.astype(qT_ref.dtype)
                ka_ref[h, c] = jnp.where(low, kc, k_aug).astype(ka_ref.dtype)
        vT_ref[h, 0, :V_DIM, :] = uv_ref[:, cols].T.astype(vT_ref.dtype)
        vT_ref[h, 0, V_DIM:, :] = ones_rows


def _qkv_prep(uq, uk, uv, gq, gk, scales, stacked):
    b, seq, _ = uq.shape
    blk = _attn_block(seq)
    nblk = seq // blk
    width = QK_STACK if stacked else V_DIM
    dtype = F8 if stacked else BF16
    in_spec = pl.BlockSpec((None, blk, D_ATTN), lambda bi, i: (bi, i, 0))
    return pl.pallas_call(
        functools.partial(_qkv_prep_kernel, blk, stacked),
        grid=(b, nblk),
        in_specs=[pl.BlockSpec(memory_space=pltpu.SMEM), in_spec, in_spec, in_spec,
                  _resident(gq.shape), _resident(gk.shape)],
        out_specs=[pl.BlockSpec((None, N_HEADS, 2, width, blk), lambda bi, i: (bi, 0, 0, 0, i)),
                   pl.BlockSpec((None, N_HEADS, 2, blk, width), lambda bi, i: (bi, 0, 0, i, 0)),
                   pl.BlockSpec((None, N_HEADS, 1, V_ROWS, blk), lambda bi, i: (bi, 0, i, 0, 0))],
        out_shape=[jax.ShapeDtypeStruct((b, N_HEADS, 2, width, seq), dtype),
                   jax.ShapeDtypeStruct((b, N_HEADS, 2, seq, width), dtype),
                   jax.ShapeDtypeStruct((b, N_HEADS, nblk, V_ROWS, blk), BF16)],
        compiler_params=pltpu.CompilerParams(dimension_semantics=("arbitrary", "arbitrary"),
                                             vmem_limit_bytes=VMEM_LIMIT_BYTES),
        name="qkv_prep",
    )(scales, uq, uk, uv, gq, gk)


def _attn_kernel(lambda_init, unshifted, slopes_ref, qT_ref, ka_ref, vT_ref, lq1_ref, lk1_ref,
                 lq2_ref, lk2_ref, sub_ref, out_ref, qv_ref, m_ref, acc_ref):
    h = pl.program_id(1)
    i = pl.program_id(2)
    nk = ka_ref.shape[1]
    tk = ka_ref.shape[2]
    kw, tq = qT_ref.shape[1], qT_ref.shape[2]
    scale = SCORE_SCALE if unshifted else 1.0
    slope = slopes_ref[h] * scale
    aug_row = kw - HEAD_DIM

    row = lax.broadcasted_iota(jnp.int32, (kw, tq), 0)
    for c in range(2):
        q = qT_ref[c].astype(F32)
        qv_ref[0, c] = jnp.where(row < aug_row, q, -q).astype(qv_ref.dtype)
        qv_ref[1, c] = jnp.where(row < aug_row, q, 0.0).astype(qv_ref.dtype)
        qv_ref[2, c] = qT_ref[c]
    acc_ref[...] = jnp.zeros(acc_ref.shape, F32)

    def diag_unit():
        kk = lax.broadcasted_iota(jnp.int32, (tk, tq), 0)
        qq = lax.broadcasted_iota(jnp.int32, (tk, tq), 1)
        return i, 1, slope * jnp.abs(kk - qq).astype(F32)

    def off_diag_unit(t):
        j = jnp.where(t >= i, t + 1, t)
        dist = jnp.abs(j - i) * tk
        return j, jnp.where(j > i, 2, 0), slope * jnp.full((1, tq), dist, jnp.int32).astype(F32)

    def scores(unit, c):
        j, side, bias = unit
        return jnp.dot(ka_ref[c, j], qv_ref[side, c], preferred_element_type=F32) - bias

    def unshifted_chunk(units):
        chains = [(u, c) for u in units for c in range(2)]
        probs = [None] * len(chains)
        pv = [[], []]

        def score_stage(k):
            probs[k] = jnp.exp2(scores(*chains[k]) * (LOG2_E / scale)).astype(BF16)

        def pv_stage(k):
            (j, _, _), c = chains[k]
            pv[c].append(jnp.dot(vT_ref[j], probs[k], preferred_element_type=F32))

        score_stage(0)
        for k in range(1, len(chains)):
            score_stage(k)
            pv_stage(k - 1)
        pv_stage(len(chains) - 1)
        for c in range(2):
            acc_ref[c] += functools.reduce(lambda a, b: a + b, pv[c])

    def running_max_step(unit):
        for c in range(2):
            s = scores(unit, c)
            m_old = m_ref[c]
            m_new = jnp.maximum(m_old, jnp.max(s, axis=0, keepdims=True))
            p = jnp.exp(s - m_new).astype(BF16)
            alpha = jnp.exp(m_old - m_new)
            acc_ref[c] = acc_ref[c] * alpha + jnp.dot(vT_ref[unit[0]], p, preferred_element_type=F32)
            m_ref[c] = m_new

    if unshifted:
        chunk = min(UNSHIFTED_CHUNK, nk)
        unshifted_chunk([diag_unit()] + [off_diag_unit(t) for t in range(chunk - 1)])

        def body(g, carry):
            t0 = chunk - 1 + g * chunk
            unshifted_chunk([off_diag_unit(t0 + r) for r in range(chunk)])
            return carry

        lax.fori_loop(0, nk // chunk - 1, body, 0)
    else:
        m_ref[...] = jnp.full(m_ref.shape, NEG_BIG, F32)
        running_max_step(diag_unit())

        def body(t, carry):
            running_max_step(off_diag_unit(t))
            return carry

        lax.fori_loop(0, nk - 1, body, 0)

    lam = (jnp.exp(jnp.sum(lq1_ref[...] * lk1_ref[...], axis=-1, keepdims=True))
           - jnp.exp(jnp.sum(lq2_ref[...] * lk2_ref[...], axis=-1, keepdims=True))
           + lambda_init)
    a0 = acc_ref[0]
    a1 = acc_ref[1]
    o = a0[:V_DIM] / a0[V_DIM:V_DIM + 1] - lam * (a1[:V_DIM] / a1[V_DIM:V_DIM + 1])
    ms = jnp.mean(o * o, axis=0, keepdims=True)
    y = o * lax.rsqrt(ms + NORM_EPS) * sub_ref[...] * (1.0 - lambda_init)
    out_ref[...] = y.T.astype(out_ref.dtype)


def _attention(qT, ka, vT, lq1, lk1, lq2, lk2, sub_col, lambda_init, unshifted):
    b, nh, _, kw, seq = qT.shape
    blk = vT.shape[-1]
    nblk = seq // blk
    assert nblk % min(UNSHIFTED_CHUNK, nblk) == 0
    ka = ka.reshape(b, nh, 2, nblk, blk, kw)
    slopes = jnp.asarray([_slope(h) for h in range(nh)], F32)
    return pl.pallas_call(
        functools.partial(_attn_kernel, lambda_init, unshifted),
        grid=(b, nh, nblk),
        in_specs=[pl.BlockSpec(memory_space=pltpu.SMEM),
                  pl.BlockSpec((None, None, 2, kw, blk), lambda bi, h, i: (bi, h, 0, 0, i)),
                  pl.BlockSpec((None, None, 2, nblk, blk, kw), lambda bi, h, i: (bi, h, 0, 0, 0, 0)),
                  pl.BlockSpec((None, None, nblk, V_ROWS, blk), lambda bi, h, i: (bi, h, 0, 0, 0)),
                  _resident(lq1.shape), _resident(lk1.shape), _resident(lq2.shape),
                  _resident(lk2.shape), _resident(sub_col.shape)],
        out_specs=pl.BlockSpec((None, blk, V_DIM), lambda bi, h, i: (bi, i, h)),
        out_shape=jax.ShapeDtypeStruct((b, seq, nh * V_DIM), BF16),
        scratch_shapes=[pltpu.VMEM((3, 2, kw, blk), qT.dtype),
                        pltpu.VMEM((2, 1, blk), F32),
                        pltpu.VMEM((2, V_ROWS, blk), F32)],
        compiler_params=pltpu.CompilerParams(
            dimension_semantics=("arbitrary", "arbitrary", "arbitrary"),
            vmem_limit_bytes=VMEM_LIMIT_BYTES),
        name="attention_unshifted" if unshifted else "attention_running_max",
    )(slopes, qT, ka, vT, lq1, lk1, lq2, lk2, sub_col)


def _diff_attention(uq, uk, uv, q_gain, k_gain, lq1, lk1, lq2, lk2, sub_col, lambda_init):
    gq = jnp.max(jnp.abs(q_gain))
    gk = jnp.max(jnp.abs(k_gain))
    score_bound = (1.01 * HEAD_DIM ** 0.5) * gq * gk
    q_scale = jnp.exp2(jnp.floor(jnp.log2(F8_TARGET_MAX / jnp.maximum(gq, 1e-30))))
    scales = jnp.stack([q_scale, SCORE_SCALE / q_scale]).astype(F32)
    gq2 = jnp.concatenate([q_gain, q_gain]).reshape(1, -1).astype(F32)
    gk2 = jnp.concatenate([k_gain, k_gain]).reshape(1, -1).astype(F32)

    def branch(unshifted):
        def run(uq, uk, uv):
            qT, ka, vT = _qkv_prep(uq, uk, uv, gq2, gk2, scales, unshifted)
            return _attention(qT, ka, vT, lq1, lk1, lq2, lk2, sub_col, lambda_init, unshifted)
        return run

    return lax.cond(score_bound <= MAX_UNSHIFTED_SCORE, branch(True), branch(False), uq, uk, uv)


def _out_ffn2_kernel(x1_ref, ycp_ref, yat_ref, wo_cp_ref, wo_at_ref, g2_ref, wg_ref, wu_ref,
                     wd_ref, gp_ref, out_ref):
    x2 = (x1_ref[...]
          + jnp.dot(ycp_ref[...], wo_cp_ref[...], preferred_element_type=F32)
          + jnp.dot(yat_ref[...], wo_at_ref[...], preferred_element_type=F32))
    x3 = x2 + 0.5 * _swiglu(_rms(x2, g2_ref[...]), wg_ref, wu_ref, wd_ref)
    out_ref[...] = _rms(x3, gp_ref[...])


def _out_ffn2(x1, ycp, yat, wo_cp, wo_at, g2, wg, wu, wd, gp):
    rows, d = x1.shape
    f = wg.shape[1]
    tm = _row_block(rows)
    row_spec = lambda w: pl.BlockSpec((tm, w), lambda i: (i, 0))
    return pl.pallas_call(
        _out_ffn2_kernel,
        grid=(rows // tm,),
        in_specs=[row_spec(d), row_spec(ycp.shape[1]), row_spec(yat.shape[1]),
                  _resident(wo_cp.shape), _resident(wo_at.shape), _resident((1, d)),
                  _resident((d, f)), _resident((d, f)), _resident((f, d)), _resident((1, d))],
        out_specs=row_spec(d),
        out_shape=jax.ShapeDtypeStruct((rows, d), F32),
        compiler_params=pltpu.CompilerParams(dimension_semantics=("arbitrary",),
                                             vmem_limit_bytes=VMEM_LIMIT_BYTES),
        name="out_ffn2",
    )(x1, ycp, yat, wo_cp, wo_at, g2, wg, wu, wd, gp)


def _pool_block_diag(pool_w):
    g, cg, _ = pool_w.shape
    out = jnp.zeros((g * cg, g * cg), pool_w.dtype)
    for k in range(g):
        out = out.at[k * cg:(k + 1) * cg, k * cg:(k + 1) * cg].set(pool_w[k])
    return out


def kernel(x, ffn1_norm, ffn1_w_gate, ffn1_w_up, ffn1_w_down, mix_norm, w_in, conv_dw, conv_dw_bias, conv_ln_gain, conv_ln_bias, pool_w, pool_scale, q_norm, k_norm, lambda_q1, lambda_k1, lambda_q2, lambda_k2, attn_subln, w_out, ffn2_norm, ffn2_w_gate, ffn2_w_up, ffn2_w_down, post_norm):
    b, seq, d = x.shape
    depth = w_in.shape[0]
    rows = b * seq
    d_cp = D_CONV + D_POOL
    row = lambda v: v.reshape(1, -1).astype(F32)
    xf = x.reshape(rows, d).astype(F32)
    for l in range(depth):
        lambda_init = 0.8 - 0.6 * math.exp(-0.3 * l)
        x1, ucp, uq, uk, uv = _ffn1_win(
            xf, row(ffn1_norm[l]), ffn1_w_gate[l].astype(BF16), ffn1_w_up[l].astype(BF16),
            ffn1_w_down[l].astype(BF16), row(mix_norm[l]), w_in[l].astype(BF16))
        ycp = _convpool(
            ucp.reshape(b, seq, -1), conv_dw[l].astype(F32), row(conv_dw_bias[l]),
            row(conv_ln_gain[l]), row(conv_ln_bias[l]),
            _pool_block_diag(pool_w[l]).astype(BF16), row(pool_scale[l]))
        yat = _diff_attention(
            uq.reshape(b, seq, -1), uk.reshape(b, seq, -1), uv.reshape(b, seq, -1),
            q_norm[l], k_norm[l], row(lambda_q1[l]), row(lambda_k1[l]), row(lambda_q2[l]),
            row(lambda_k2[l]), attn_subln[l].reshape(-1, 1).astype(F32), lambda_init)
        wo = w_out[l].astype(BF16)
        xf = _out_ffn2(x1, ycp.reshape(rows, d_cp), yat.reshape(rows, -1), wo[:d_cp], wo[d_cp:],
                       row(ffn2_norm[l]), ffn2_w_gate[l].astype(BF16), ffn2_w_up[l].astype(BF16),
                       ffn2_w_down[l].astype(BF16), row(post_norm[l]))
    return xf.reshape(b, seq, d).astype(x.dtype)
```

```python
import functools
import math

import jax
import jax.numpy as jnp
from jax import lax
from jax.experimental import pallas as pl
from jax.experimental.pallas import tpu as pltpu

F32 = jnp.float32
BF16 = jnp.bfloat16

NORM_EPS = 1e-6
N_HEADS = 4
HEAD_DIM = 64
V_DIM = 2 * HEAD_DIM
D_CONV = 256
D_POOL = 256
D_ATTN = 512
CONV_WIDTH = 31
POOL_GROUP_DIM = 64
SUBLANES = 8
HALO = 16
V_ROWS = V_DIM + 8
BF16_EXACT_INT = 256
VMEM_LIMIT_BYTES = 56 * 1024 * 1024
NEG_BIG = -1e30
MAX_UNSHIFTED_SCORE = 60.0
UNSHIFTED_CHUNK = 16


def _attn_block(seq):
    return min(512, seq)


def _row_block(rows):
    return min(256, rows)


def _rms(x, g):
    ms = jnp.mean(x * x, axis=-1, keepdims=True)
    return x * lax.rsqrt(ms + NORM_EPS) * g


def _sigmoid(x):
    return 1.0 / (1.0 + jnp.exp(-x))


def _swiglu(h, wg_ref, wu_ref, wd_ref):
    hb = h.astype(BF16)
    g = jnp.dot(hb, wg_ref[...], preferred_element_type=F32)
    u = jnp.dot(hb, wu_ref[...], preferred_element_type=F32)
    a = (g * _sigmoid(g) * u).astype(BF16)
    return jnp.dot(a, wd_ref[...], preferred_element_type=F32)


def _resident(shape):
    nd = len(shape)
    return pl.BlockSpec(shape, lambda *_: (0,) * nd, pipeline_mode=pl.Buffered(1))


def _ffn1_win_kernel(x_ref, g1_ref, wg_ref, wu_ref, wd_ref, gm_ref, win_ref,
                     x1_ref, ucp_ref, uq_ref, uk_ref, uv_ref):
    x = x_ref[...]
    x1 = x + 0.5 * _swiglu(_rms(x, g1_ref[...]), wg_ref, wu_ref, wd_ref)
    x1_ref[...] = x1
    hm = _rms(x1, gm_ref[...]).astype(BF16)
    u = jnp.dot(hm, win_ref[...], preferred_element_type=F32)
    c0 = 2 * D_CONV + D_POOL
    ucp_ref[...] = u[:, :c0]
    uq_ref[...] = u[:, c0:c0 + D_ATTN]
    uk_ref[...] = u[:, c0 + D_ATTN:c0 + 2 * D_ATTN]
    uv_ref[...] = u[:, c0 + 2 * D_ATTN:]


def _ffn1_win(x, g1, wg, wu, wd, gm, win):
    rows, d = x.shape
    f = wg.shape[1]
    tm = _row_block(rows)
    c0 = 2 * D_CONV + D_POOL
    row_spec = lambda w: pl.BlockSpec((tm, w), lambda i: (i, 0))
    return pl.pallas_call(
        _ffn1_win_kernel,
        grid=(rows // tm,),
        in_specs=[row_spec(d), _resident((1, d)), _resident((d, f)), _resident((d, f)),
                  _resident((f, d)), _resident((1, d)), _resident(win.shape)],
        out_specs=[row_spec(d), row_spec(c0), row_spec(D_ATTN), row_spec(D_ATTN), row_spec(D_ATTN)],
        out_shape=[jax.ShapeDtypeStruct((rows, d), F32),
                   jax.ShapeDtypeStruct((rows, c0), F32),
                   jax.ShapeDtypeStruct((rows, D_ATTN), F32),
                   jax.ShapeDtypeStruct((rows, D_ATTN), F32),
                   jax.ShapeDtypeStruct((rows, D_ATTN), F32)],
        compiler_params=pltpu.CompilerParams(dimension_semantics=("arbitrary",),
                                             vmem_limit_bytes=VMEM_LIMIT_BYTES),
        name="ffn1_win",
    )(x, g1, wg, wu, wd, gm, win)


def _convpool_kernel(seq, main_ref, prev_ref, next_ref, dw_ref, dwb_ref, lng_ref, lnb_ref,
                     pw_ref, ps_ref, out_ref, zext, zrot, pext, a1, a2, a3):
    ts = main_ref.shape[0]
    i = pl.program_id(1)
    has_prev = i > 0
    has_next = i < pl.num_programs(1) - 1

    def glu(u):
        return u[:, :D_CONV] * _sigmoid(u[:, D_CONV:2 * D_CONV])

    um = main_ref[...]
    up = prev_ref[...]
    un = next_ref[...]
    zero_halo = jnp.zeros((HALO, D_CONV), F32)
    zext[0:HALO, :] = jnp.where(has_prev, glu(up), zero_halo)
    zext[HALO:HALO + ts, :] = glu(um)
    zext[HALO + ts:, :] = jnp.where(has_next, glu(un), zero_halo)
    pext[0:HALO, :] = jnp.where(has_prev, up[:, 2 * D_CONV:], zero_halo)
    pext[HALO:HALO + ts, :] = um[:, 2 * D_CONV:]
    pext[HALO + ts:, :] = jnp.where(has_next, un[:, 2 * D_CONV:], zero_halo)

    off0 = HALO - CONV_WIDTH // 2
    nz = ts + 2 * HALO - SUBLANES
    for r in range(1, SUBLANES):
        zrot[r - 1, 0:nz, :] = zext[r:r + nz, :]
    acc = jnp.zeros((ts, D_CONV), F32)
    for j in range(CONV_WIDTH):
        r, a = (off0 + j) % SUBLANES, (off0 + j) // SUBLANES * SUBLANES
        tap = zext[a:a + ts, :] if r == 0 else zrot[r - 1, a:a + ts, :]
        acc = acc + dw_ref[j:j + 1, :] * tap
    z = acc + dwb_ref[...]
    mu = jnp.mean(z, axis=-1, keepdims=True)
    zc = z - mu
    var = jnp.mean(zc * zc, axis=-1, keepdims=True)
    y = zc * lax.rsqrt(var + NORM_EPS) * lng_ref[...] + lnb_ref[...]
    out_ref[:, :D_CONV] = (y * _sigmoid(y)).astype(out_ref.dtype)

    n = ts + 2 * HALO
    a1[1:n, :] = pext[0:n - 1, :] + pext[1:n, :]
    a2[2:n - 1, :] = a1[1:n - 2, :] + a1[3:n, :]
    a3[4:n - 3, :] = a2[2:n - 5, :] + a2[6:n - 1, :]
    w2 = a1[HALO:HALO + ts, :]
    w4 = a2[HALO:HALO + ts, :]
    w8 = a3[HALO:HALO + ts, :]
    w16 = a3[HALO - 4:HALO - 4 + ts, :] + a3[HALO + 4:HALO + 4 + ts, :]
    grp = lax.broadcasted_iota(jnp.int32, (ts, D_POOL), 1) // POOL_GROUP_DIM
    win = jnp.where(grp == 0, w2, jnp.where(grp == 1, w4, jnp.where(grp == 2, w8, w16)))
    t = lax.broadcasted_iota(jnp.int32, (ts, D_POOL), 0) + i * ts
    half = jnp.left_shift(1, grp)
    cnt = jnp.minimum(t + half, seq) - jnp.maximum(t - half, 0)
    d = win / cnt.astype(F32) - pext[HALO:HALO + ts, :]
    yp = jnp.dot(d.astype(BF16), pw_ref[...], preferred_element_type=F32) * ps_ref[...]
    out_ref[:, D_CONV:] = yp.astype(out_ref.dtype)


def _convpool(ucp, dw, dwb, lng, lnb, pw_bd, ps):
    b, seq, c0 = ucp.shape
    ts = _attn_block(seq)
    nh = ts // HALO
    last_halo = seq // HALO - 1
    n = ts + 2 * HALO
    return pl.pallas_call(
        functools.partial(_convpool_kernel, seq),
        grid=(b, seq // ts),
        in_specs=[pl.BlockSpec((None, ts, c0), lambda bi, i: (bi, i, 0)),
                  pl.BlockSpec((None, HALO, c0), lambda bi, i: (bi, jnp.maximum(i * nh - 1, 0), 0)),
                  pl.BlockSpec((None, HALO, c0),
                               lambda bi, i: (bi, jnp.minimum((i + 1) * nh, last_halo), 0)),
                  _resident(dw.shape), _resident(dwb.shape), _resident(lng.shape),
                  _resident(lnb.shape), _resident(pw_bd.shape), _resident(ps.shape)],
        out_specs=pl.BlockSpec((None, ts, D_CONV + D_POOL), lambda bi, i: (bi, i, 0)),
        out_shape=jax.ShapeDtypeStruct((b, seq, D_CONV + D_POOL), BF16),
        scratch_shapes=[pltpu.VMEM((n, D_CONV), F32), pltpu.VMEM((SUBLANES - 1, n, D_CONV), F32),
                        pltpu.VMEM((n, D_POOL), F32),
                        pltpu.VMEM((n, D_POOL), F32), pltpu.VMEM((n, D_POOL), F32),
                        pltpu.VMEM((n, D_POOL), F32)],
        compiler_params=pltpu.CompilerParams(dimension_semantics=("arbitrary", "arbitrary"),
                                             vmem_limit_bytes=VMEM_LIMIT_BYTES),
        name="convpool",
    )(ucp, ucp, ucp, dw, dwb, lng, lnb, pw_bd, ps)


def _slope(h):
    return 2.0 ** (-8.0 * (h + 1) / N_HEADS)


def _qkv_prep_kernel(blk, uq_ref, uk_ref, uv_ref, gq_ref, gk_ref, qT_ref, ka_ref, vT_ref):
    ts = uq_ref.shape[0]
    i = pl.program_id(1)
    lane = lax.broadcasted_iota(jnp.int32, (ts, V_DIM), 1)
    low = lane < HEAD_DIM
    pos = (lax.broadcasted_iota(jnp.int32, (ts, V_DIM), 0) + i * ts) % blk
    one = jnp.ones((ts, V_DIM), F32)
    zero = jnp.zeros((ts, V_DIM), F32)
    row8 = lax.broadcasted_iota(jnp.int32, (V_ROWS - V_DIM, ts), 0)
    ones_rows = jnp.where(row8 == 0, 1.0, 0.0).astype(vT_ref.dtype)
    digits = [(pos % BF16_EXACT_INT).astype(F32)]
    place = BF16_EXACT_INT
    while place < blk:
        digits.append((pos % (place * BF16_EXACT_INT) - pos % place).astype(F32))
        place *= BF16_EXACT_INT

    def norm_pair(x, g):
        x2 = x * x
        s_lo = jnp.sum(jnp.where(low, x2, 0.0), axis=-1, keepdims=True)
        s_hi = jnp.sum(jnp.where(low, 0.0, x2), axis=-1, keepdims=True)
        ms = jnp.where(low, s_lo, s_hi) * (1.0 / HEAD_DIM)
        return x * lax.rsqrt(ms + NORM_EPS) * g

    def aug(cols):
        a = zero
        for n, col in enumerate(cols):
            a = jnp.where(lane == HEAD_DIM + n, col, a)
        return a

    def other_map(x):
        return pltpu.roll(x, HEAD_DIM, 1)

    for h in range(N_HEADS):
        cols = slice(h * V_DIM, (h + 1) * V_DIM)
        qn = norm_pair(uq_ref[:, cols], gq_ref[...]) * (HEAD_DIM ** -0.5)
        kn = norm_pair(uk_ref[:, cols], gk_ref[...])
        coef = _slope(h)
        q_aug = aug([-coef * one] * len(digits) + digits)
        k_aug = aug(digits + [coef * one] * len(digits))
        for c in range(2):
            qc = qn if c == 0 else other_map(qn)
            kc = kn if c == 0 else other_map(kn)
            qT_ref[h, c] = jnp.where(low, qc, q_aug).T.astype(qT_ref.dtype)
            ka_ref[h, c] = jnp.where(low, kc, k_aug).astype(ka_ref.dtype)
        vT_ref[h, 0, :V_DIM, :] = uv_ref[:, cols].T.astype(vT_ref.dtype)
        vT_ref[h, 0, V_DIM:, :] = ones_rows


def _qkv_prep(uq, uk, uv, gq, gk):
    b, seq, _ = uq.shape
    blk = _attn_block(seq)
    nblk = seq // blk
    in_spec = pl.BlockSpec((None, blk, D_ATTN), lambda bi, i: (bi, i, 0))
    return pl.pallas_call(
        functools.partial(_qkv_prep_kernel, blk),
        grid=(b, nblk),
        in_specs=[in_spec, in_spec, in_spec, _resident(gq.shape), _resident(gk.shape)],
        out_specs=[pl.BlockSpec((None, N_HEADS, 2, V_DIM, blk), lambda bi, i: (bi, 0, 0, 0, i)),
                   pl.BlockSpec((None, N_HEADS, 2, blk, V_DIM), lambda bi, i: (bi, 0, 0, i, 0)),
                   pl.BlockSpec((None, N_HEADS, 1, V_ROWS, blk), lambda bi, i: (bi, 0, i, 0, 0))],
        out_shape=[jax.ShapeDtypeStruct((b, N_HEADS, 2, V_DIM, seq), BF16),
                   jax.ShapeDtypeStruct((b, N_HEADS, 2, seq, V_DIM), BF16),
                   jax.ShapeDtypeStruct((b, N_HEADS, nblk, V_ROWS, blk), BF16)],
        compiler_params=pltpu.CompilerParams(dimension_semantics=("arbitrary", "arbitrary"),
                                             vmem_limit_bytes=VMEM_LIMIT_BYTES),
        name="qkv_prep",
    )(uq, uk, uv, gq, gk)


def _attn_kernel(lambda_init, unshifted, slopes_ref, qT_ref, ka_ref, vT_ref, lq1_ref, lk1_ref,
                 lq2_ref, lk2_ref, sub_ref, out_ref, qv_ref, m_ref, acc_ref):
    h = pl.program_id(1)
    i = pl.program_id(2)
    nk = ka_ref.shape[1]
    tk = ka_ref.shape[2]
    tq = qT_ref.shape[2]
    slope = slopes_ref[h]

    row = lax.broadcasted_iota(jnp.int32, (V_DIM, tq), 0)
    for c in range(2):
        q = qT_ref[c]
        qv_ref[0, c] = jnp.where(row < HEAD_DIM, q, -q)
        qv_ref[1, c] = jnp.where(row < HEAD_DIM, q, jnp.zeros_like(q))
        qv_ref[2, c] = q
    acc_ref[...] = jnp.zeros(acc_ref.shape, F32)

    def diag_unit():
        kk = lax.broadcasted_iota(jnp.int32, (tk, tq), 0)
        qq = lax.broadcasted_iota(jnp.int32, (tk, tq), 1)
        return i, 1, slope * jnp.abs(kk - qq).astype(F32)

    def off_diag_unit(t):
        j = jnp.where(t >= i, t + 1, t)
        dist = jnp.abs(j - i) * tk
        return j, jnp.where(j > i, 2, 0), slope * jnp.full((1, tq), dist, jnp.int32).astype(F32)

    def scores(unit, c):
        j, side, bias = unit
        return jnp.dot(ka_ref[c, j], qv_ref[side, c], preferred_element_type=F32) - bias

    def unshifted_chunk(units):
        chains = [(u, c) for u in units for c in range(2)]
        probs = [None] * len(chains)
        pv = [[], []]

        def score_stage(k):
            probs[k] = jnp.exp(scores(*chains[k])).astype(BF16)

        def pv_stage(k):
            (j, _, _), c = chains[k]
            pv[c].append(jnp.dot(vT_ref[j], probs[k], preferred_element_type=F32))

        score_stage(0)
        for k in range(1, len(chains)):
            score_stage(k)
            pv_stage(k - 1)
        pv_stage(len(chains) - 1)
        for c in range(2):
            acc_ref[c] += functools.reduce(lambda a, b: a + b, pv[c])

    def running_max_step(unit):
        for c in range(2):
            s = scores(unit, c)
            m_old = m_ref[c]
            m_new = jnp.maximum(m_old, jnp.max(s, axis=0, keepdims=True))
            p = jnp.exp(s - m_new).astype(BF16)
            alpha = jnp.exp(m_old - m_new)
            acc_ref[c] = acc_ref[c] * alpha + jnp.dot(vT_ref[unit[0]], p, preferred_element_type=F32)
            m_ref[c] = m_new

    if unshifted:
        chunk = min(UNSHIFTED_CHUNK, nk)
        unshifted_chunk([diag_unit()] + [off_diag_unit(t) for t in range(chunk - 1)])

        def body(g, carry):
            t0 = chunk - 1 + g * chunk
            unshifted_chunk([off_diag_unit(t0 + r) for r in range(chunk)])
            return carry

        lax.fori_loop(0, nk // chunk - 1, body, 0)
    else:
        m_ref[...] = jnp.full(m_ref.shape, NEG_BIG, F32)
        running_max_step(diag_unit())

        def body(t, carry):
            running_max_step(off_diag_unit(t))
            return carry

        lax.fori_loop(0, nk - 1, body, 0)

    lam = (jnp.exp(jnp.sum(lq1_ref[...] * lk1_ref[...], axis=-1, keepdims=True))
           - jnp.exp(jnp.sum(lq2_ref[...] * lk2_ref[...], axis=-1, keepdims=True))
           + lambda_init)
    a0 = acc_ref[0]
    a1 = acc_ref[1]
    o = a0[:V_DIM] / a0[V_DIM:V_DIM + 1] - lam * (a1[:V_DIM] / a1[V_DIM:V_DIM + 1])
    ms = jnp.mean(o * o, axis=0, keepdims=True)
    y = o * lax.rsqrt(ms + NORM_EPS) * sub_ref[...] * (1.0 - lambda_init)
    out_ref[...] = y.T.astype(out_ref.dtype)


def _attention(qT, ka, vT, lq1, lk1, lq2, lk2, sub_col, lambda_init, unshifted):
    b, nh, _, kw, seq = qT.shape
    blk = vT.shape[-1]
    nblk = seq // blk
    assert kw == V_DIM and nblk % min(UNSHIFTED_CHUNK, nblk) == 0
    ka = ka.reshape(b, nh, 2, nblk, blk, kw)
    slopes = jnp.asarray([_slope(h) for h in range(nh)], F32)
    return pl.pallas_call(
        functools.partial(_attn_kernel, lambda_init, unshifted),
        grid=(b, nh, nblk),
        in_specs=[pl.BlockSpec(memory_space=pltpu.SMEM),
                  pl.BlockSpec((None, None, 2, kw, blk), lambda bi, h, i: (bi, h, 0, 0, i)),
                  pl.BlockSpec((None, None, 2, nblk, blk, kw), lambda bi, h, i: (bi, h, 0, 0, 0, 0)),
                  pl.BlockSpec((None, None, nblk, V_ROWS, blk), lambda bi, h, i: (bi, h, 0, 0, 0)),
                  _resident(lq1.shape), _resident(lk1.shape), _resident(lq2.shape),
                  _resident(lk2.shape), _resident(sub_col.shape)],
        out_specs=pl.BlockSpec((None, blk, V_DIM), lambda bi, h, i: (bi, i, h)),
        out_shape=jax.ShapeDtypeStruct((b, seq, nh * V_DIM), BF16),
        scratch_shapes=[pltpu.VMEM((3, 2, kw, blk), qT.dtype),
                        pltpu.VMEM((2, 1, blk), F32),
                        pltpu.VMEM((2, V_ROWS, blk), F32)],
        compiler_params=pltpu.CompilerParams(
            dimension_semantics=("arbitrary", "arbitrary", "arbitrary"),
            vmem_limit_bytes=VMEM_LIMIT_BYTES),
        name="attention_unshifted" if unshifted else "attention_running_max",
    )(slopes, qT, ka, vT, lq1, lk1, lq2, lk2, sub_col)


def _diff_attention(uq, uk, uv, q_gain, k_gain, lq1, lk1, lq2, lk2, sub_col, lambda_init):
    gq2 = jnp.concatenate([q_gain, q_gain]).reshape(1, -1).astype(F32)
    gk2 = jnp.concatenate([k_gain, k_gain]).reshape(1, -1).astype(F32)
    qT, ka, vT = _qkv_prep(uq, uk, uv, gq2, gk2)
    score_bound = (1.01 * HEAD_DIM ** 0.5) * jnp.max(jnp.abs(q_gain)) * jnp.max(jnp.abs(k_gain))

    def branch(unshifted):
        return lambda qT, ka, vT: _attention(qT, ka, vT, lq1, lk1, lq2, lk2, sub_col, lambda_init,
                                             unshifted)

    return lax.cond(score_bound <= MAX_UNSHIFTED_SCORE, branch(True), branch(False), qT, ka, vT)


def _out_ffn2_kernel(x1_ref, ycp_ref, yat_ref, wo_cp_ref, wo_at_ref, g2_ref, wg_ref, wu_ref,
                     wd_ref, gp_ref, out_ref):
    x2 = (x1_ref[...]
          + jnp.dot(ycp_ref[...], wo_cp_ref[...], preferred_element_type=F32)
          + jnp.dot(yat_ref[...], wo_at_ref[...], preferred_element_type=F32))
    x3 = x2 + 0.5 * _swiglu(_rms(x2, g2_ref[...]), wg_ref, wu_ref, wd_ref)
    out_ref[...] = _rms(x3, gp_ref[...])


def _out_ffn2(x1, ycp, yat, wo_cp, wo_at, g2, wg, wu, wd, gp):
    rows, d = x1.shape
    f = wg.shape[1]
    tm = _row_block(rows)
    row_spec = lambda w: pl.BlockSpec((tm, w), lambda i: (i, 0))
    return pl.pallas_call(
        _out_ffn2_kernel,
        grid=(rows // tm,),
        in_specs=[row_spec(d), row_spec(ycp.shape[1]), row_spec(yat.shape[1]),
                  _resident(wo_cp.shape), _resident(wo_at.shape), _resident((1, d)),
                  _resident((d, f)), _resident((d, f)), _resident((f, d)), _resident((1, d))],
        out_specs=row_spec(d),
        out_shape=jax.ShapeDtypeStruct((rows, d), F32),
        compiler_params=pltpu.CompilerParams(dimension_semantics=("arbitrary",),
                                             vmem_limit_bytes=VMEM_LIMIT_BYTES),
        name="out_ffn2",
    )(x1, ycp, yat, wo_cp, wo_at, g2, wg, wu, wd, gp)


def _pool_block_diag(pool_w):
    g, cg, _ = pool_w.shape
    out = jnp.zeros((g * cg, g * cg), pool_w.dtype)
    for k in range(g):
        out = out.at[k * cg:(k + 1) * cg, k * cg:(k + 1) * cg].set(pool_w[k])
    return out


def kernel(x, ffn1_norm, ffn1_w_gate, ffn1_w_up, ffn1_w_down, mix_norm, w_in, conv_dw, conv_dw_bias, conv_ln_gain, conv_ln_bias, pool_w, pool_scale, q_norm, k_norm, lambda_q1, lambda_k1, lambda_q2, lambda_k2, attn_subln, w_out, ffn2_norm, ffn2_w_gate, ffn2_w_up, ffn2_w_down, post_norm):
    b, seq, d = x.shape
    depth = w_in.shape[0]
    rows = b * seq
    d_cp = D_CONV + D_POOL
    row = lambda v: v.reshape(1, -1).astype(F32)
    xf = x.reshape(rows, d).astype(F32)
    for l in range(depth):
        lambda_init = 0.8 - 0.6 * math.exp(-0.3 * l)
        x1, ucp, uq, uk, uv = _ffn1_win(
            xf, row(ffn1_norm[l]), ffn1_w_gate[l].astype(BF16), ffn1_w_up[l].astype(BF16),
            ffn1_w_down[l].astype(BF16), row(mix_norm[l]), w_in[l].astype(BF16))
        ycp = _convpool(
            ucp.reshape(b, seq, -1), conv_dw[l].astype(F32), row(conv_dw_bias[l]),
            row(conv_ln_gain[l]), row(conv_ln_bias[l]),
            _pool_block_diag(pool_w[l]).astype(BF16), row(pool_scale[l]))
        yat = _diff_attention(
            uq.reshape(b, seq, -1), uk.reshape(b, seq, -1), uv.reshape(b, seq, -1),
            q_norm[l], k_norm[l], row(lambda_q1[l]), row(lambda_k1[l]), row(lambda_q2[l]),
            row(lambda_k2[l]), attn_subln[l].reshape(-1, 1).astype(F32), lambda_init)
        wo = w_out[l].astype(BF16)
        xf = _out_ffn2(x1, ycp.reshape(rows, d_cp), yat.reshape(rows, -1), wo[:d_cp], wo[d_cp:],
                       row(ffn2_norm[l]), ffn2_w_gate[l].astype(BF16), ffn2_w_up[l].astype(BF16),
                       ffn2_w_down[l].astype(BF16), row(post_norm[l]))
    return xf.reshape(b, seq, d).astype(x.dtype)
```

```python
import functools
import math

import jax
import jax.numpy as jnp
from jax import lax
from jax.experimental import pallas as pl
from jax.experimental.pallas import tpu as pltpu

F32 = jnp.float32
BF16 = jnp.bfloat16

NORM_EPS = 1e-6
N_HEADS = 4
HEAD_DIM = 64
V_DIM = 2 * HEAD_DIM
D_CONV = 256
D_POOL = 256
D_ATTN = 512
CONV_WIDTH = 31
POOL_GROUP_DIM = 64
SUBLANES = 8
HALO = 16
V_ROWS = V_DIM + 8
BF16_EXACT_INT = 256
VMEM_LIMIT_BYTES = 56 * 1024 * 1024
NEG_BIG = -1e30
MAX_UNSHIFTED_SCORE = 60.0
UNSHIFTED_CHUNK = 16


def _attn_block(seq):
    return min(512, seq)


def _row_block(rows):
    return min(512, rows)


def _rms(x, g):
    ms = jnp.mean(x * x, axis=-1, keepdims=True)
    return x * lax.rsqrt(ms + NORM_EPS) * g


def _sigmoid(x):
    return 1.0 / (1.0 + jnp.exp(-x))


def _row_halves(tm):
    half = tm // 2 if tm % (2 * SUBLANES) == 0 else tm
    return [slice(r, r + half) for r in range(0, tm, half)]


def _swiglu_halves(hs, wg_ref, wu_ref, wd_ref):
    hbs = [h.astype(BF16) for h in hs]
    gus = [(jnp.dot(hb, wg_ref[...], preferred_element_type=F32),
            jnp.dot(hb, wu_ref[...], preferred_element_type=F32)) for hb in hbs]
    acts = [(g * _sigmoid(g) * u).astype(BF16) for g, u in gus]
    return [jnp.dot(a, wd_ref[...], preferred_element_type=F32) for a in acts]


def _resident(shape):
    nd = len(shape)
    return pl.BlockSpec(shape, lambda *_: (0,) * nd, pipeline_mode=pl.Buffered(1))


def _slope(h):
    return 2.0 ** (-8.0 * (h + 1) / N_HEADS)


def _attention_operands(uq, uk, uv, rows, gq_ref, gk_ref, qT_ref, ka_ref, vT_ref):
    n = uq.shape[0]
    blk = qT_ref.shape[-1]
    lane = lax.broadcasted_iota(jnp.int32, (n, V_DIM), 1)
    low = lane < HEAD_DIM
    pos = lax.broadcasted_iota(jnp.int32, (n, V_DIM), 0) + rows.start
    one = jnp.ones((n, V_DIM), F32)
    zero = jnp.zeros((n, V_DIM), F32)
    row8 = lax.broadcasted_iota(jnp.int32, (V_ROWS - V_DIM, n), 0)
    ones_rows = jnp.where(row8 == 0, 1.0, 0.0).astype(vT_ref.dtype)
    digits = [(pos % BF16_EXACT_INT).astype(F32)]
    place = BF16_EXACT_INT
    while place < blk:
        digits.append((pos % (place * BF16_EXACT_INT) - pos % place).astype(F32))
        place *= BF16_EXACT_INT

    def norm_pair(x, g):
        x2 = x * x
        s_lo = jnp.sum(jnp.where(low, x2, 0.0), axis=-1, keepdims=True)
        s_hi = jnp.sum(jnp.where(low, 0.0, x2), axis=-1, keepdims=True)
        ms = jnp.where(low, s_lo, s_hi) * (1.0 / HEAD_DIM)
        return x * lax.rsqrt(ms + NORM_EPS) * g

    def aug(cols):
        a = zero
        for k, col in enumerate(cols):
            a = jnp.where(lane == HEAD_DIM + k, col, a)
        return a

    def other_map(x):
        return pltpu.roll(x, HEAD_DIM, 1)

    for h in range(N_HEADS):
        cols = slice(h * V_DIM, (h + 1) * V_DIM)
        qn = norm_pair(uq[:, cols], gq_ref[...]) * (HEAD_DIM ** -0.5)
        kn = norm_pair(uk[:, cols], gk_ref[...])
        coef = _slope(h)
        q_aug = aug([-coef * one] * len(digits) + digits)
        k_aug = aug(digits + [coef * one] * len(digits))
        for c in range(2):
            qc = qn if c == 0 else other_map(qn)
            kc = kn if c == 0 else other_map(kn)
            qT_ref[h, c, :, rows] = jnp.where(low, qc, q_aug).T.astype(qT_ref.dtype)
            ka_ref[h, c, rows, :] = jnp.where(low, kc, k_aug).astype(ka_ref.dtype)
        vT_ref[h, :V_DIM, rows] = uv[:, cols].T.astype(vT_ref.dtype)
        vT_ref[h, V_DIM:, rows] = ones_rows


def _ffn1_win_kernel(x_ref, g1_ref, wg_ref, wu_ref, wd_ref, gm_ref, win_ref, gq_ref, gk_ref,
                     x1_ref, ucp_ref, qT_ref, ka_ref, vT_ref):
    halves = _row_halves(x_ref.shape[0])
    xs = [x_ref[r, :] for r in halves]
    ys = _swiglu_halves([_rms(x, g1_ref[...]) for x in xs], wg_ref, wu_ref, wd_ref)
    x1s = [x + 0.5 * y for x, y in zip(xs, ys)]
    hms = [_rms(x1, gm_ref[...]).astype(BF16) for x1 in x1s]
    c0 = 2 * D_CONV + D_POOL
    u_attn = [jnp.dot(hm, win_ref[:, c0:], preferred_element_type=F32) for hm in hms]
    for r, u in zip(halves, u_attn):
        _attention_operands(u[:, :D_ATTN], u[:, D_ATTN:2 * D_ATTN], u[:, 2 * D_ATTN:], r,
                            gq_ref, gk_ref, qT_ref, ka_ref, vT_ref)
    for r, x1, hm in zip(halves, x1s, hms):
        x1_ref[r, :] = x1
        ucp_ref[r, :] = jnp.dot(hm, win_ref[:, :c0], preferred_element_type=F32)


def _ffn1_win(x, g1, wg, wu, wd, gm, win, gq, gk, seq):
    rows, d = x.shape
    f = wg.shape[1]
    tm = _row_block(rows)
    c0 = 2 * D_CONV + D_POOL
    nblk = seq // tm
    assert tm == _attn_block(seq) and rows % seq == 0
    row_spec = lambda w: pl.BlockSpec((tm, w), lambda i: (i, 0))
    return pl.pallas_call(
        _ffn1_win_kernel,
        grid=(rows // tm,),
        in_specs=[row_spec(d), _resident((1, d)), _resident((d, f)), _resident((d, f)),
                  _resident((f, d)), _resident((1, d)), _resident(win.shape),
                  _resident(gq.shape), _resident(gk.shape)],
        out_specs=[row_spec(d), row_spec(c0),
                   pl.BlockSpec((None, N_HEADS, 2, V_DIM, tm), lambda i: (i // nblk, 0, 0, 0, i % nblk)),
                   pl.BlockSpec((None, N_HEADS, 2, tm, V_DIM), lambda i: (i // nblk, 0, 0, i % nblk, 0)),
                   pl.BlockSpec((None, N_HEADS, None, V_ROWS, tm),
                                lambda i: (i // nblk, 0, i % nblk, 0, 0))],
        out_shape=[jax.ShapeDtypeStruct((rows, d), F32),
                   jax.ShapeDtypeStruct((rows, c0), F32),
                   jax.ShapeDtypeStruct((rows // seq, N_HEADS, 2, V_DIM, seq), BF16),
                   jax.ShapeDtypeStruct((rows // seq, N_HEADS, 2, seq, V_DIM), BF16),
                   jax.ShapeDtypeStruct((rows // seq, N_HEADS, nblk, V_ROWS, tm), BF16)],
        compiler_params=pltpu.CompilerParams(dimension_semantics=("arbitrary",),
                                             vmem_limit_bytes=VMEM_LIMIT_BYTES),
        name="ffn1_win",
    )(x, g1, wg, wu, wd, gm, win, gq, gk)


def _convpool_kernel(seq, main_ref, prev_ref, next_ref, dw_ref, dwb_ref, lng_ref, lnb_ref,
                     pw_ref, ps_ref, out_ref, zext, zrot, pext, a1, a2, a3):
    ts = main_ref.shape[0]
    i = pl.program_id(1)
    has_prev = i > 0
    has_next = i < pl.num_programs(1) - 1

    def glu(u):
        return u[:, :D_CONV] * _sigmoid(u[:, D_CONV:2 * D_CONV])

    um = main_ref[...]
    up = prev_ref[...]
    un = next_ref[...]
    zero_halo = jnp.zeros((HALO, D_CONV), F32)
    zext[0:HALO, :] = jnp.where(has_prev, glu(up), zero_halo)
    zext[HALO:HALO + ts, :] = glu(um)
    zext[HALO + ts:, :] = jnp.where(has_next, glu(un), zero_halo)
    pext[0:HALO, :] = jnp.where(has_prev, up[:, 2 * D_CONV:], zero_halo)
    pext[HALO:HALO + ts, :] = um[:, 2 * D_CONV:]
    pext[HALO + ts:, :] = jnp.where(has_next, un[:, 2 * D_CONV:], zero_halo)

    off0 = HALO - CONV_WIDTH // 2
    nz = ts + 2 * HALO - SUBLANES
    for r in range(1, SUBLANES):
        zrot[r - 1, 0:nz, :] = zext[r:r + nz, :]
    acc = jnp.zeros((ts, D_CONV), F32)
    for j in range(CONV_WIDTH):
        r, a = (off0 + j) % SUBLANES, (off0 + j) // SUBLANES * SUBLANES
        tap = zext[a:a + ts, :] if r == 0 else zrot[r - 1, a:a + ts, :]
        acc = acc + dw_ref[j:j + 1, :] * tap
    z = acc + dwb_ref[...]
    mu = jnp.mean(z, axis=-1, keepdims=True)
    zc = z - mu
    var = jnp.mean(zc * zc, axis=-1, keepdims=True)
    y = zc * lax.rsqrt(var + NORM_EPS) * lng_ref[...] + lnb_ref[...]
    out_ref[:, :D_CONV] = (y * _sigmoid(y)).astype(out_ref.dtype)

    n = ts + 2 * HALO
    a1[1:n, :] = pext[0:n - 1, :] + pext[1:n, :]
    a2[2:n - 1, :] = a1[1:n - 2, :] + a1[3:n, :]
    a3[4:n - 3, :] = a2[2:n - 5, :] + a2[6:n - 1, :]
    w2 = a1[HALO:HALO + ts, :]
    w4 = a2[HALO:HALO + ts, :]
    w8 = a3[HALO:HALO + ts, :]
    w16 = a3[HALO - 4:HALO - 4 + ts, :] + a3[HALO + 4:HALO + 4 + ts, :]
    grp = lax.broadcasted_iota(jnp.int32, (ts, D_POOL), 1) // POOL_GROUP_DIM
    win = jnp.where(grp == 0, w2, jnp.where(grp == 1, w4, jnp.where(grp == 2, w8, w16)))
    t = lax.broadcasted_iota(jnp.int32, (ts, D_POOL), 0) + i * ts
    half = jnp.left_shift(1, grp)
    cnt = jnp.minimum(t + half, seq) - jnp.maximum(t - half, 0)
    d = win / cnt.astype(F32) - pext[HALO:HALO + ts, :]
    yp = jnp.dot(d.astype(BF16), pw_ref[...], preferred_element_type=F32) * ps_ref[...]
    out_ref[:, D_CONV:] = yp.astype(out_ref.dtype)


def _convpool(ucp, dw, dwb, lng, lnb, pw_bd, ps):
    b, seq, c0 = ucp.shape
    ts = _attn_block(seq)
    nh = ts // HALO
    last_halo = seq // HALO - 1
    n = ts + 2 * HALO
    return pl.pallas_call(
        functools.partial(_convpool_kernel, seq),
        grid=(b, seq // ts),
        in_specs=[pl.BlockSpec((None, ts, c0), lambda bi, i: (bi, i, 0)),
                  pl.BlockSpec((None, HALO, c0), lambda bi, i: (bi, jnp.maximum(i * nh - 1, 0), 0)),
                  pl.BlockSpec((None, HALO, c0),
                               lambda bi, i: (bi, jnp.minimum((i + 1) * nh, last_halo), 0)),
                  _resident(dw.shape), _resident(dwb.shape), _resident(lng.shape),
                  _resident(lnb.shape), _resident(pw_bd.shape), _resident(ps.shape)],
        out_specs=pl.BlockSpec((None, ts, D_CONV + D_POOL), lambda bi, i: (bi, i, 0)),
        out_shape=jax.ShapeDtypeStruct((b, seq, D_CONV + D_POOL), BF16),
        scratch_shapes=[pltpu.VMEM((n, D_CONV), F32), pltpu.VMEM((SUBLANES - 1, n, D_CONV), F32),
                        pltpu.VMEM((n, D_POOL), F32),
                        pltpu.VMEM((n, D_POOL), F32), pltpu.VMEM((n, D_POOL), F32),
                        pltpu.VMEM((n, D_POOL), F32)],
        compiler_params=pltpu.CompilerParams(dimension_semantics=("arbitrary", "arbitrary"),
                                             vmem_limit_bytes=VMEM_LIMIT_BYTES),
        name="convpool",
    )(ucp, ucp, ucp, dw, dwb, lng, lnb, pw_bd, ps)


def _attn_kernel(lambda_init, unshifted, slopes_ref, qT_ref, ka_ref, vT_ref, lq1_ref, lk1_ref,
                 lq2_ref, lk2_ref, sub_ref, out_ref, qv_ref, m_ref, acc_ref):
    h = pl.program_id(1)
    i = pl.program_id(2)
    nk = ka_ref.shape[1]
    tk = ka_ref.shape[2]
    tq = qT_ref.shape[2]
    slope = slopes_ref[h]

    row = lax.broadcasted_iota(jnp.int32, (V_DIM, tq), 0)
    for c in range(2):
        q = qT_ref[c]
        qv_ref[0, c] = jnp.where(row < HEAD_DIM, q, -q)
        qv_ref[1, c] = jnp.where(row < HEAD_DIM, q, jnp.zeros_like(q))
        qv_ref[2, c] = q
    acc_ref[...] = jnp.zeros(acc_ref.shape, F32)

    def diag_unit():
        kk = lax.broadcasted_iota(jnp.int32, (tk, tq), 0)
        qq = lax.broadcasted_iota(jnp.int32, (tk, tq), 1)
        return i, 1, slope * jnp.abs(kk - qq).astype(F32)

    def off_diag_unit(t):
        j = jnp.where(t >= i, t + 1, t)
        dist = jnp.abs(j - i) * tk
        return j, jnp.where(j > i, 2, 0), slope * jnp.full((1, tq), dist, jnp.int32).astype(F32)

    def scores(unit, c):
        j, side, bias = unit
        return jnp.dot(ka_ref[c, j], qv_ref[side, c], preferred_element_type=F32) - bias

    def unshifted_chunk(units):
        chains = [(u, c) for u in units for c in range(2)]
        probs = [None] * len(chains)
        pv = [[], []]

        def score_stage(k):
            probs[k] = jnp.exp(scores(*chains[k])).astype(BF16)

        def pv_stage(k):
            (j, _, _), c = chains[k]
            pv[c].append(jnp.dot(vT_ref[j], probs[k], preferred_element_type=F32))

        score_stage(0)
        for k in range(1, len(chains)):
            score_stage(k)
            pv_stage(k - 1)
        pv_stage(len(chains) - 1)
        for c in range(2):
            acc_ref[c] += functools.reduce(lambda a, b: a + b, pv[c])

    def running_max_step(unit):
        for c in range(2):
            s = scores(unit, c)
            m_old = m_ref[c]
            m_new = jnp.maximum(m_old, jnp.max(s, axis=0, keepdims=True))
            p = jnp.exp(s - m_new).astype(BF16)
            alpha = jnp.exp(m_old - m_new)
            acc_ref[c] = acc_ref[c] * alpha + jnp.dot(vT_ref[unit[0]], p, preferred_element_type=F32)
            m_ref[c] = m_new

    if unshifted:
        chunk = min(UNSHIFTED_CHUNK, nk)
        unshifted_chunk([diag_unit()] + [off_diag_unit(t) for t in range(chunk - 1)])

        def body(g, carry):
            t0 = chunk - 1 + g * chunk
            unshifted_chunk([off_diag_unit(t0 + r) for r in range(chunk)])
            return carry

        lax.fori_loop(0, nk // chunk - 1, body, 0)
    else:
        m_ref[...] = jnp.full(m_ref.shape, NEG_BIG, F32)
        running_max_step(diag_unit())

        def body(t, carry):
            running_max_step(off_diag_unit(t))
            return carry

        lax.fori_loop(0, nk - 1, body, 0)

    lam = (jnp.exp(jnp.sum(lq1_ref[...] * lk1_ref[...], axis=-1, keepdims=True))
           - jnp.exp(jnp.sum(lq2_ref[...] * lk2_ref[...], axis=-1, keepdims=True))
           + lambda_init)
    a0 = acc_ref[0]
    a1 = acc_ref[1]
    o = a0[:V_DIM] / a0[V_DIM:V_DIM + 1] - lam * (a1[:V_DIM] / a1[V_DIM:V_DIM + 1])
    ms = jnp.mean(o * o, axis=0, keepdims=True)
    y = o * lax.rsqrt(ms + NORM_EPS) * sub_ref[...] * (1.0 - lambda_init)
    out_ref[...] = y.T.astype(out_ref.dtype)


def _attention(qT, ka, vT, lq1, lk1, lq2, lk2, sub_col, lambda_init, unshifted):
    b, nh, _, kw, seq = qT.shape
    blk = vT.shape[-1]
    nblk = seq // blk
    assert kw == V_DIM and nblk % min(UNSHIFTED_CHUNK, nblk) == 0
    ka = ka.reshape(b, nh, 2, nblk, blk, kw)
    slopes = jnp.asarray([_slope(h) for h in range(nh)], F32)
    return pl.pallas_call(
        functools.partial(_attn_kernel, lambda_init, unshifted),
        grid=(b, nh, nblk),
        in_specs=[pl.BlockSpec(memory_space=pltpu.SMEM),
                  pl.BlockSpec((None, None, 2, kw, blk), lambda bi, h, i: (bi, h, 0, 0, i)),
                  pl.BlockSpec((None, None, 2, nblk, blk, kw), lambda bi, h, i: (bi, h, 0, 0, 0, 0)),
                  pl.BlockSpec((None, None, nblk, V_ROWS, blk), lambda bi, h, i: (bi, h, 0, 0, 0)),
                  _resident(lq1.shape), _resident(lk1.shape), _resident(lq2.shape),
                  _resident(lk2.shape), _resident(sub_col.shape)],
        out_specs=pl.BlockSpec((None, blk, V_DIM), lambda bi, h, i: (bi, i, h)),
        out_shape=jax.ShapeDtypeStruct((b, seq, nh * V_DIM), BF16),
        scratch_shapes=[pltpu.VMEM((3, 2, kw, blk), qT.dtype),
                        pltpu.VMEM((2, 1, blk), F32),
                        pltpu.VMEM((2, V_ROWS, blk), F32)],
        compiler_params=pltpu.CompilerParams(
            dimension_semantics=("arbitrary", "arbitrary", "arbitrary"),
            vmem_limit_bytes=VMEM_LIMIT_BYTES),
        name="attention_unshifted" if unshifted else "attention_running_max",
    )(slopes, qT, ka, vT, lq1, lk1, lq2, lk2, sub_col)


def _diff_attention(qT, ka, vT, q_gain, k_gain, lq1, lk1, lq2, lk2, sub_col, lambda_init):
    score_bound = (1.01 * HEAD_DIM ** 0.5) * jnp.max(jnp.abs(q_gain)) * jnp.max(jnp.abs(k_gain))

    def branch(unshifted):
        return lambda qT, ka, vT: _attention(qT, ka, vT, lq1, lk1, lq2, lk2, sub_col, lambda_init,
                                             unshifted)

    return lax.cond(score_bound <= MAX_UNSHIFTED_SCORE, branch(True), branch(False), qT, ka, vT)


def _out_ffn2_kernel(x1_ref, ycp_ref, yat_ref, wo_cp_ref, wo_at_ref, g2_ref, wg_ref, wu_ref,
                     wd_ref, gp_ref, out_ref):
    halves = _row_halves(x1_ref.shape[0])
    x2s = [x1_ref[r, :]
           + jnp.dot(ycp_ref[r, :], wo_cp_ref[...], preferred_element_type=F32)
           + jnp.dot(yat_ref[r, :], wo_at_ref[...], preferred_element_type=F32) for r in halves]
    ys = _swiglu_halves([_rms(x2, g2_ref[...]) for x2 in x2s], wg_ref, wu_ref, wd_ref)
    for r, x2, y in zip(halves, x2s, ys):
        out_ref[r, :] = _rms(x2 + 0.5 * y, gp_ref[...])


def _out_ffn2(x1, ycp, yat, wo_cp, wo_at, g2, wg, wu, wd, gp):
    rows, d = x1.shape
    f = wg.shape[1]
    tm = _row_block(rows)
    row_spec = lambda w: pl.BlockSpec((tm, w), lambda i: (i, 0))
    return pl.pallas_call(
        _out_ffn2_kernel,
        grid=(rows // tm,),
        in_specs=[row_spec(d), row_spec(ycp.shape[1]), row_spec(yat.shape[1]),
                  _resident(wo_cp.shape), _resident(wo_at.shape), _resident((1, d)),
                  _resident((d, f)), _resident((d, f)), _resident((f, d)), _resident((1, d))],
        out_specs=row_spec(d),
        out_shape=jax.ShapeDtypeStruct((rows, d), F32),
        compiler_params=pltpu.CompilerParams(dimension_semantics=("arbitrary",),
                                             vmem_limit_bytes=VMEM_LIMIT_BYTES),
        name="out_ffn2",
    )(x1, ycp, yat, wo_cp, wo_at, g2, wg, wu, wd, gp)


def _pool_block_diag(pool_w):
    g, cg, _ = pool_w.shape
    out = jnp.zeros((g * cg, g * cg), pool_w.dtype)
    for k in range(g):
        out = out.at[k * cg:(k + 1) * cg, k * cg:(k + 1) * cg].set(pool_w[k])
    return out


def kernel(x, ffn1_norm, ffn1_w_gate, ffn1_w_up, ffn1_w_down, mix_norm, w_in, conv_dw, conv_dw_bias, conv_ln_gain, conv_ln_bias, pool_w, pool_scale, q_norm, k_norm, lambda_q1, lambda_k1, lambda_q2, lambda_k2, attn_subln, w_out, ffn2_norm, ffn2_w_gate, ffn2_w_up, ffn2_w_down, post_norm):
    b, seq, d = x.shape
    depth = w_in.shape[0]
    rows = b * seq
    d_cp = D_CONV + D_POOL
    row = lambda v: v.reshape(1, -1).astype(F32)
    xf = x.reshape(rows, d).astype(F32)
    for l in range(depth):
        lambda_init = 0.8 - 0.6 * math.exp(-0.3 * l)
        pair = lambda g: jnp.concatenate([g, g]).reshape(1, -1).astype(F32)
        x1, ucp, qT, ka, vT = _ffn1_win(
            xf, row(ffn1_norm[l]), ffn1_w_gate[l].astype(BF16), ffn1_w_up[l].astype(BF16),
            ffn1_w_down[l].astype(BF16), row(mix_norm[l]), w_in[l].astype(BF16),
            pair(q_norm[l]), pair(k_norm[l]), seq)
        ycp = _convpool(
            ucp.reshape(b, seq, -1), conv_dw[l].astype(F32), row(conv_dw_bias[l]),
            row(conv_ln_gain[l]), row(conv_ln_bias[l]),
            _pool_block_diag(pool_w[l]).astype(BF16), row(pool_scale[l]))
        yat = _diff_attention(
            qT, ka, vT, q_norm[l], k_norm[l], row(lambda_q1[l]), row(lambda_k1[l]), row(lambda_q2[l]),
            row(lambda_k2[l]), attn_subln[l].reshape(-1, 1).astype(F32), lambda_init)
        wo = w_out[l].astype(BF16)
        xf = _out_ffn2(x1, ycp.reshape(rows, d_cp), yat.reshape(rows, -1), wo[:d_cp], wo[d_cp:],
                       row(ffn2_norm[l]), ffn2_w_gate[l].astype(BF16), ffn2_w_up[l].astype(BF16),
                       ffn2_w_down[l].astype(BF16), row(post_norm[l]))
    return xf.reshape(b, seq, d).astype(x.dtype)
```

```python
import functools
import math

import jax
import jax.numpy as jnp
from jax import lax
from jax.experimental import pallas as pl
from jax.experimental.pallas import tpu as pltpu

F32 = jnp.float32
BF16 = jnp.bfloat16

NORM_EPS = 1e-6
N_HEADS = 4
HEAD_DIM = 64
V_DIM = 2 * HEAD_DIM
D_CONV = 256
D_POOL = 256
D_ATTN = 512
CONV_WIDTH = 31
POOL_GROUP_DIM = 64
SUBLANES = 8
HALO = 16
V_ROWS = V_DIM + 8
BF16_EXACT_INT = 256
VMEM_LIMIT_BYTES = 56 * 1024 * 1024
NEG_BIG = -1e30
MAX_UNSHIFTED_SCORE = 60.0
KEY_SUB = 256
PV_LAG = 2
UNSHIFTED_CHUNK = 16


def _attn_block(seq):
    return min(512, seq)


def _row_block(rows):
    return min(512, rows)


def _rms(x, g):
    ms = jnp.mean(x * x, axis=-1, keepdims=True)
    return x * lax.rsqrt(ms + NORM_EPS) * g


def _sigmoid(x):
    return 1.0 / (1.0 + jnp.exp(-x))


def _row_halves(tm):
    half = tm // 2 if tm % (2 * SUBLANES) == 0 else tm
    return [slice(r, r + half) for r in range(0, tm, half)]


def _swiglu_halves(hs, wg_ref, wu_ref, wd_ref):
    hbs = [h.astype(BF16) for h in hs]
    gus = [(jnp.dot(hb, wg_ref[...], preferred_element_type=F32),
            jnp.dot(hb, wu_ref[...], preferred_element_type=F32)) for hb in hbs]
    acts = [(g * _sigmoid(g) * u).astype(BF16) for g, u in gus]
    return [jnp.dot(a, wd_ref[...], preferred_element_type=F32) for a in acts]


def _resident(shape):
    nd = len(shape)
    return pl.BlockSpec(shape, lambda *_: (0,) * nd, pipeline_mode=pl.Buffered(1))


def _slope(h):
    return 2.0 ** (-8.0 * (h + 1) / N_HEADS)


def _attention_operands(uq, uk, uv, rows, gq_ref, gk_ref, qT_ref, ka_ref, vT_ref):
    n = uq.shape[0]
    blk = qT_ref.shape[-1]
    lane = lax.broadcasted_iota(jnp.int32, (n, V_DIM), 1)
    low = lane < HEAD_DIM
    pos = lax.broadcasted_iota(jnp.int32, (n, V_DIM), 0) + rows.start
    one = jnp.ones((n, V_DIM), F32)
    zero = jnp.zeros((n, V_DIM), F32)
    row8 = lax.broadcasted_iota(jnp.int32, (V_ROWS - V_DIM, n), 0)
    ones_rows = jnp.where(row8 == 0, 1.0, 0.0).astype(vT_ref.dtype)
    digits = [(pos % BF16_EXACT_INT).astype(F32)]
    place = BF16_EXACT_INT
    while place < blk:
        digits.append((pos % (place * BF16_EXACT_INT) - pos % place).astype(F32))
        place *= BF16_EXACT_INT

    def norm_pair(x, g):
        x2 = x * x
        s_lo = jnp.sum(jnp.where(low, x2, 0.0), axis=-1, keepdims=True)
        s_hi = jnp.sum(jnp.where(low, 0.0, x2), axis=-1, keepdims=True)
        ms = jnp.where(low, s_lo, s_hi) * (1.0 / HEAD_DIM)
        return x * lax.rsqrt(ms + NORM_EPS) * g

    def aug(cols):
        a = zero
        for k, col in enumerate(cols):
            a = jnp.where(lane == HEAD_DIM + k, col, a)
        return a

    def other_map(x):
        return pltpu.roll(x, HEAD_DIM, 1)

    for h in range(N_HEADS):
        cols = slice(h * V_DIM, (h + 1) * V_DIM)
        qn = norm_pair(uq[:, cols], gq_ref[...]) * (HEAD_DIM ** -0.5)
        kn = norm_pair(uk[:, cols], gk_ref[...])
        coef = _slope(h)
        q_aug = aug([-coef * one] * len(digits) + digits)
        k_aug = aug(digits + [coef * one] * len(digits))
        for c in range(2):
            qc = qn if c == 0 else other_map(qn)
            kc = kn if c == 0 else other_map(kn)
            qT_ref[h, c, :, rows] = jnp.where(low, qc, q_aug).T.astype(qT_ref.dtype)
            ka_ref[h, c, rows, :] = jnp.where(low, kc, k_aug).astype(ka_ref.dtype)
        vT_ref[h, :V_DIM, rows] = uv[:, cols].T.astype(vT_ref.dtype)
        vT_ref[h, V_DIM:, rows] = ones_rows


def _ffn1_win_kernel(x_ref, g1_ref, wg_ref, wu_ref, wd_ref, gm_ref, win_ref, gq_ref, gk_ref,
                     x1_ref, ucp_ref, qT_ref, ka_ref, vT_ref):
    halves = _row_halves(x_ref.shape[0])
    xs = [x_ref[r, :] for r in halves]
    ys = _swiglu_halves([_rms(x, g1_ref[...]) for x in xs], wg_ref, wu_ref, wd_ref)
    x1s = [x + 0.5 * y for x, y in zip(xs, ys)]
    hms = [_rms(x1, gm_ref[...]).astype(BF16) for x1 in x1s]
    c0 = 2 * D_CONV + D_POOL
    u_attn = [jnp.dot(hm, win_ref[:, c0:], preferred_element_type=F32) for hm in hms]
    for r, u in zip(halves, u_attn):
        _attention_operands(u[:, :D_ATTN], u[:, D_ATTN:2 * D_ATTN], u[:, 2 * D_ATTN:], r,
                            gq_ref, gk_ref, qT_ref, ka_ref, vT_ref)
    for r, x1, hm in zip(halves, x1s, hms):
        x1_ref[r, :] = x1
        ucp_ref[r, :] = jnp.dot(hm, win_ref[:, :c0], preferred_element_type=F32)


def _ffn1_win(x, g1, wg, wu, wd, gm, win, gq, gk, seq):
    rows, d = x.shape
    f = wg.shape[1]
    tm = _row_block(rows)
    c0 = 2 * D_CONV + D_POOL
    nblk = seq // tm
    assert tm == _attn_block(seq) and rows % seq == 0
    row_spec = lambda w: pl.BlockSpec((tm, w), lambda i: (i, 0))
    return pl.pallas_call(
        _ffn1_win_kernel,
        grid=(rows // tm,),
        in_specs=[row_spec(d), _resident((1, d)), _resident((d, f)), _resident((d, f)),
                  _resident((f, d)), _resident((1, d)), _resident(win.shape),
                  _resident(gq.shape), _resident(gk.shape)],
        out_specs=[row_spec(d), row_spec(c0),
                   pl.BlockSpec((None, N_HEADS, 2, V_DIM, tm), lambda i: (i // nblk, 0, 0, 0, i % nblk)),
                   pl.BlockSpec((None, N_HEADS, 2, tm, V_DIM), lambda i: (i // nblk, 0, 0, i % nblk, 0)),
                   pl.BlockSpec((None, N_HEADS, None, V_ROWS, tm),
                                lambda i: (i // nblk, 0, i % nblk, 0, 0))],
        out_shape=[jax.ShapeDtypeStruct((rows, d), F32),
                   jax.ShapeDtypeStruct((rows, c0), F32),
                   jax.ShapeDtypeStruct((rows // seq, N_HEADS, 2, V_DIM, seq), BF16),
                   jax.ShapeDtypeStruct((rows // seq, N_HEADS, 2, seq, V_DIM), BF16),
                   jax.ShapeDtypeStruct((rows // seq, N_HEADS, nblk, V_ROWS, tm), BF16)],
        compiler_params=pltpu.CompilerParams(dimension_semantics=("arbitrary",),
                                             vmem_limit_bytes=VMEM_LIMIT_BYTES),
        name="ffn1_win",
    )(x, g1, wg, wu, wd, gm, win, gq, gk)


def _convpool_tile(t0, seq, main_ref, prev_ref, next_ref, params, scratch, out_ref):
    dw_ref, dwb_ref, lng_ref, lnb_ref, pw_ref, ps_ref = params
    zext, zrot, pext, a1, a2, a3 = scratch
    ts = main_ref.shape[0]
    has_prev = t0 > 0
    has_next = t0 + ts < seq

    def glu(u):
        return u[:, :D_CONV] * _sigmoid(u[:, D_CONV:2 * D_CONV])

    def assemble():
        um = main_ref[...]
        up = prev_ref[...]
        un = next_ref[...]
        zero_halo = jnp.zeros((HALO, D_CONV), F32)
        zext[0:HALO, :] = jnp.where(has_prev, glu(up), zero_halo)
        zext[HALO:HALO + ts, :] = glu(um)
        zext[HALO + ts:, :] = jnp.where(has_next, glu(un), zero_halo)
        pext[0:HALO, :] = jnp.where(has_prev, up[:, 2 * D_CONV:], zero_halo)
        pext[HALO:HALO + ts, :] = um[:, 2 * D_CONV:]
        pext[HALO + ts:, :] = jnp.where(has_next, un[:, 2 * D_CONV:], zero_halo)

    def conv():
        off0 = HALO - CONV_WIDTH // 2
        nz = ts + 2 * HALO - SUBLANES
        for r in range(1, SUBLANES):
            zrot[r - 1, 0:nz, :] = zext[r:r + nz, :]
        acc = jnp.zeros((ts, D_CONV), F32)
        for j in range(CONV_WIDTH):
            r, a = (off0 + j) % SUBLANES, (off0 + j) // SUBLANES * SUBLANES
            tap = zext[a:a + ts, :] if r == 0 else zrot[r - 1, a:a + ts, :]
            acc = acc + dw_ref[j:j + 1, :] * tap
        z = acc + dwb_ref[...]
        mu = jnp.mean(z, axis=-1, keepdims=True)
        zc = z - mu
        var = jnp.mean(zc * zc, axis=-1, keepdims=True)
        y = zc * lax.rsqrt(var + NORM_EPS) * lng_ref[...] + lnb_ref[...]
        out_ref[:, :D_CONV] = (y * _sigmoid(y)).astype(out_ref.dtype)

    def pool():
        n = ts + 2 * HALO
        a1[1:n, :] = pext[0:n - 1, :] + pext[1:n, :]
        a2[2:n - 1, :] = a1[1:n - 2, :] + a1[3:n, :]
        a3[4:n - 3, :] = a2[2:n - 5, :] + a2[6:n - 1, :]
        w2 = a1[HALO:HALO + ts, :]
        w4 = a2[HALO:HALO + ts, :]
        w8 = a3[HALO:HALO + ts, :]
        w16 = a3[HALO - 4:HALO - 4 + ts, :] + a3[HALO + 4:HALO + 4 + ts, :]
        grp = lax.broadcasted_iota(jnp.int32, (ts, D_POOL), 1) // POOL_GROUP_DIM
        win = jnp.where(grp == 0, w2, jnp.where(grp == 1, w4, jnp.where(grp == 2, w8, w16)))
        t = lax.broadcasted_iota(jnp.int32, (ts, D_POOL), 0) + t0
        half = jnp.left_shift(1, grp)
        cnt = jnp.minimum(t + half, seq) - jnp.maximum(t - half, 0)
        d = win / cnt.astype(F32) - pext[HALO:HALO + ts, :]
        yp = jnp.dot(d.astype(BF16), pw_ref[...], preferred_element_type=F32) * ps_ref[...]
        out_ref[:, D_CONV:] = yp.astype(out_ref.dtype)

    return [assemble, conv, pool]


def _convpool_kernel(seq, main_ref, prev_ref, next_ref, dw_ref, dwb_ref, lng_ref, lnb_ref,
                     pw_ref, ps_ref, out_ref, *scratch):
    t0 = pl.program_id(1) * main_ref.shape[0]
    for stage in _convpool_tile(t0, seq, main_ref, prev_ref, next_ref,
                                (dw_ref, dwb_ref, lng_ref, lnb_ref, pw_ref, ps_ref), scratch, out_ref):
        stage()


def _convpool(ucp, dw, dwb, lng, lnb, pw_bd, ps):
    b, seq, c0 = ucp.shape
    ts = _attn_block(seq)
    nh = ts // HALO
    last_halo = seq // HALO - 1
    n = ts + 2 * HALO
    return pl.pallas_call(
        functools.partial(_convpool_kernel, seq),
        grid=(b, seq // ts),
        in_specs=[pl.BlockSpec((None, ts, c0), lambda bi, i: (bi, i, 0)),
                  pl.BlockSpec((None, HALO, c0), lambda bi, i: (bi, jnp.maximum(i * nh - 1, 0), 0)),
                  pl.BlockSpec((None, HALO, c0),
                               lambda bi, i: (bi, jnp.minimum((i + 1) * nh, last_halo), 0)),
                  _resident(dw.shape), _resident(dwb.shape), _resident(lng.shape),
                  _resident(lnb.shape), _resident(pw_bd.shape), _resident(ps.shape)],
        out_specs=pl.BlockSpec((None, ts, D_CONV + D_POOL), lambda bi, i: (bi, i, 0)),
        out_shape=jax.ShapeDtypeStruct((b, seq, D_CONV + D_POOL), BF16),
        scratch_shapes=[pltpu.VMEM((n, D_CONV), F32), pltpu.VMEM((SUBLANES - 1, n, D_CONV), F32),
                        pltpu.VMEM((n, D_POOL), F32), pltpu.VMEM((n, D_POOL), F32),
                        pltpu.VMEM((n, D_POOL), F32), pltpu.VMEM((n, D_POOL), F32)],
        compiler_params=pltpu.CompilerParams(dimension_semantics=("arbitrary", "arbitrary"),
                                             vmem_limit_bytes=VMEM_LIMIT_BYTES),
        name="convpool",
    )(ucp, ucp, ucp, dw, dwb, lng, lnb, pw_bd, ps)


def _attn_kernel(lambda_init, unshifted, slopes_ref, qT_ref, ka_ref, vT_ref, lq1_ref, lk1_ref,
                 lq2_ref, lk2_ref, sub_ref, out_ref, qv_ref, m_ref, acc_ref):
    h = pl.program_id(1)
    i = pl.program_id(2)
    nk = ka_ref.shape[1]
    tk = ka_ref.shape[2]
    tq = qT_ref.shape[2]
    slope = slopes_ref[h]

    row = lax.broadcasted_iota(jnp.int32, (V_DIM, tq), 0)
    for c in range(2):
        q = qT_ref[c]
        qv_ref[0, c] = jnp.where(row < HEAD_DIM, q, -q)
        qv_ref[1, c] = jnp.where(row < HEAD_DIM, q, jnp.zeros_like(q))
        qv_ref[2, c] = q
    acc_ref[...] = jnp.zeros(acc_ref.shape, F32)

    def diag_unit():
        kk = lax.broadcasted_iota(jnp.int32, (tk, tq), 0)
        qq = lax.broadcasted_iota(jnp.int32, (tk, tq), 1)
        return i, 1, slope * jnp.abs(kk - qq).astype(F32)

    def off_diag_unit(t):
        j = jnp.where(t >= i, t + 1, t)
        dist = jnp.abs(j - i) * tk
        return j, jnp.where(j > i, 2, 0), slope * jnp.full((1, tq), dist, jnp.int32).astype(F32)

    def scores(unit, c):
        j, side, bias = unit
        return jnp.dot(ka_ref[c, j], qv_ref[side, c], preferred_element_type=F32) - bias

    def unshifted_chunk(units):
        chains = [(u, c, r0) for u in units for c in range(2) for r0 in range(0, tk, KEY_SUB)]
        probs = [None] * len(chains)
        pv = [[], []]

        def score_stage(k):
            (j, side, bias), c, r0 = chains[k]
            if bias.shape[0] != 1:
                bias = bias[r0:r0 + KEY_SUB]
            s = jnp.dot(ka_ref[c, j, r0:r0 + KEY_SUB, :], qv_ref[side, c], preferred_element_type=F32)
            probs[k] = jnp.exp(s - bias).astype(BF16)

        def pv_stage(k):
            (j, _, _), c, r0 = chains[k]
            pv[c].append(jnp.dot(vT_ref[j, :, r0:r0 + KEY_SUB], probs[k], preferred_element_type=F32))

        for k in range(len(chains) + PV_LAG):
            if k < len(chains):
                score_stage(k)
            if k >= PV_LAG:
                pv_stage(k - PV_LAG)
        for c in range(2):
            acc_ref[c] += functools.reduce(lambda a, b: a + b, pv[c])

    def running_max_step(unit):
        for c in range(2):
            s = scores(unit, c)
            m_old = m_ref[c]
            m_new = jnp.maximum(m_old, jnp.max(s, axis=0, keepdims=True))
            p = jnp.exp(s - m_new).astype(BF16)
            alpha = jnp.exp(m_old - m_new)
            acc_ref[c] = acc_ref[c] * alpha + jnp.dot(vT_ref[unit[0]], p, preferred_element_type=F32)
            m_ref[c] = m_new

    if unshifted:
        chunk = min(UNSHIFTED_CHUNK, nk)
        unshifted_chunk([diag_unit()] + [off_diag_unit(t) for t in range(chunk - 1)])

        def body(g, carry):
            t0 = chunk - 1 + g * chunk
            unshifted_chunk([off_diag_unit(t0 + r) for r in range(chunk)])
            return carry

        lax.fori_loop(0, nk // chunk - 1, body, 0)
    else:
        m_ref[...] = jnp.full(m_ref.shape, NEG_BIG, F32)
        running_max_step(diag_unit())

        def body(t, carry):
            running_max_step(off_diag_unit(t))
            return carry

        lax.fori_loop(0, nk - 1, body, 0)

    lam = (jnp.exp(jnp.sum(lq1_ref[...] * lk1_ref[...], axis=-1, keepdims=True))
           - jnp.exp(jnp.sum(lq2_ref[...] * lk2_ref[...], axis=-1, keepdims=True))
           + lambda_init)
    a0 = acc_ref[0]
    a1 = acc_ref[1]
    o = a0[:V_DIM] / a0[V_DIM:V_DIM + 1] - lam * (a1[:V_DIM] / a1[V_DIM:V_DIM + 1])
    ms = jnp.mean(o * o, axis=0, keepdims=True)
    y = o * lax.rsqrt(ms + NORM_EPS) * sub_ref[...] * (1.0 - lambda_init)
    out_ref[...] = y.T.astype(out_ref.dtype)


def _attention(qT, ka, vT, lq1, lk1, lq2, lk2, sub_col, lambda_init, unshifted):
    b, nh, _, kw, seq = qT.shape
    blk = vT.shape[-1]
    nblk = seq // blk
    assert kw == V_DIM and nblk % min(UNSHIFTED_CHUNK, nblk) == 0 and blk % KEY_SUB == 0
    ka = ka.reshape(b, nh, 2, nblk, blk, kw)
    slopes = jnp.asarray([_slope(h) for h in range(nh)], F32)
    return pl.pallas_call(
        functools.partial(_attn_kernel, lambda_init, unshifted),
        grid=(b, nh, nblk),
        in_specs=[pl.BlockSpec(memory_space=pltpu.SMEM),
                  pl.BlockSpec((None, None, 2, kw, blk), lambda bi, h, i: (bi, h, 0, 0, i)),
                  pl.BlockSpec((None, None, 2, nblk, blk, kw), lambda bi, h, i: (bi, h, 0, 0, 0, 0)),
                  pl.BlockSpec((None, None, nblk, V_ROWS, blk), lambda bi, h, i: (bi, h, 0, 0, 0)),
                  _resident(lq1.shape), _resident(lk1.shape), _resident(lq2.shape),
                  _resident(lk2.shape), _resident(sub_col.shape)],
        out_specs=pl.BlockSpec((None, blk, V_DIM), lambda bi, h, i: (bi, i, h)),
        out_shape=jax.ShapeDtypeStruct((b, seq, nh * V_DIM), BF16),
        scratch_shapes=[pltpu.VMEM((3, 2, kw, blk), qT.dtype),
                        pltpu.VMEM((2, 1, blk), F32),
                        pltpu.VMEM((2, V_ROWS, blk), F32)],
        compiler_params=pltpu.CompilerParams(
            dimension_semantics=("arbitrary", "arbitrary", "arbitrary"),
            vmem_limit_bytes=VMEM_LIMIT_BYTES),
        name="attention_unshifted" if unshifted else "attention_running_max",
    )(slopes, qT, ka, vT, lq1, lk1, lq2, lk2, sub_col)


def _diff_attention(qT, ka, vT, q_gain, k_gain, lq1, lk1, lq2, lk2, sub_col, lambda_init):
    score_bound = (1.01 * HEAD_DIM ** 0.5) * jnp.max(jnp.abs(q_gain)) * jnp.max(jnp.abs(k_gain))

    def branch(unshifted):
        return lambda qT, ka, vT: _attention(qT, ka, vT, lq1, lk1, lq2, lk2, sub_col, lambda_init,
                                             unshifted)

    return lax.cond(score_bound <= MAX_UNSHIFTED_SCORE, branch(True), branch(False), qT, ka, vT)


def _out_ffn2_kernel(x1_ref, ycp_ref, yat_ref, wo_cp_ref, wo_at_ref, g2_ref, wg_ref, wu_ref,
                     wd_ref, gp_ref, out_ref):
    halves = _row_halves(x1_ref.shape[0])
    x2s = [x1_ref[r, :]
           + jnp.dot(ycp_ref[r, :], wo_cp_ref[...], preferred_element_type=F32)
           + jnp.dot(yat_ref[r, :], wo_at_ref[...], preferred_element_type=F32) for r in halves]
    ys = _swiglu_halves([_rms(x2, g2_ref[...]) for x2 in x2s], wg_ref, wu_ref, wd_ref)
    for r, x2, y in zip(halves, x2s, ys):
        out_ref[r, :] = _rms(x2 + 0.5 * y, gp_ref[...])


def _out_ffn2(x1, ycp, yat, wo_cp, wo_at, g2, wg, wu, wd, gp):
    rows, d = x1.shape
    f = wg.shape[1]
    tm = _row_block(rows)
    row_spec = lambda w: pl.BlockSpec((tm, w), lambda i: (i, 0))
    return pl.pallas_call(
        _out_ffn2_kernel,
        grid=(rows // tm,),
        in_specs=[row_spec(d), row_spec(ycp.shape[1]), row_spec(yat.shape[1]),
                  _resident(wo_cp.shape), _resident(wo_at.shape), _resident((1, d)),
                  _resident((d, f)), _resident((d, f)), _resident((f, d)), _resident((1, d))],
        out_specs=row_spec(d),
        out_shape=jax.ShapeDtypeStruct((rows, d), F32),
        compiler_params=pltpu.CompilerParams(dimension_semantics=("arbitrary",),
                                             vmem_limit_bytes=VMEM_LIMIT_BYTES),
        name="out_ffn2",
    )(x1, ycp, yat, wo_cp, wo_at, g2, wg, wu, wd, gp)


def _pool_block_diag(pool_w):
    g, cg, _ = pool_w.shape
    out = jnp.zeros((g * cg, g * cg), pool_w.dtype)
    for k in range(g):
        out = out.at[k * cg:(k + 1) * cg, k * cg:(k + 1) * cg].set(pool_w[k])
    return out


def kernel(x, ffn1_norm, ffn1_w_gate, ffn1_w_up, ffn1_w_down, mix_norm, w_in, conv_dw, conv_dw_bias, conv_ln_gain, conv_ln_bias, pool_w, pool_scale, q_norm, k_norm, lambda_q1, lambda_k1, lambda_q2, lambda_k2, attn_subln, w_out, ffn2_norm, ffn2_w_gate, ffn2_w_up, ffn2_w_down, post_norm):
    b, seq, d = x.shape
    depth = w_in.shape[0]
    rows = b * seq
    d_cp = D_CONV + D_POOL
    row = lambda v: v.reshape(1, -1).astype(F32)
    xf = x.reshape(rows, d).astype(F32)
    for l in range(depth):
        lambda_init = 0.8 - 0.6 * math.exp(-0.3 * l)
        pair = lambda g: jnp.concatenate([g, g]).reshape(1, -1).astype(F32)
        x1, ucp, qT, ka, vT = _ffn1_win(
            xf, row(ffn1_norm[l]), ffn1_w_gate[l].astype(BF16), ffn1_w_up[l].astype(BF16),
            ffn1_w_down[l].astype(BF16), row(mix_norm[l]), w_in[l].astype(BF16),
            pair(q_norm[l]), pair(k_norm[l]), seq)
        ycp = _convpool(
            ucp.reshape(b, seq, -1), conv_dw[l].astype(F32), row(conv_dw_bias[l]),
            row(conv_ln_gain[l]), row(conv_ln_bias[l]),
            _pool_block_diag(pool_w[l]).astype(BF16), row(pool_scale[l]))
        yat = _diff_attention(
            qT, ka, vT, q_norm[l], k_norm[l], row(lambda_q1[l]), row(lambda_k1[l]), row(lambda_q2[l]),
            row(lambda_k2[l]), attn_subln[l].reshape(-1, 1).astype(F32), lambda_init)
        wo = w_out[l].astype(BF16)
        xf = _out_ffn2(x1, ycp.reshape(rows, d_cp), yat.reshape(rows, -1), wo[:d_cp], wo[d_cp:],
                       row(ffn2_norm[l]), ffn2_w_gate[l].astype(BF16), ffn2_w_up[l].astype(BF16),
                       ffn2_w_down[l].astype(BF16), row(post_norm[l]))
    return xf.reshape(b, seq, d).astype(x.dtype)
```

```python
import functools
import math

import jax
import jax.numpy as jnp
from jax import lax
from jax.experimental import pallas as pl
from jax.experimental.pallas import tpu as pltpu

F32 = jnp.float32
BF16 = jnp.bfloat16

NORM_EPS = 1e-6
N_HEADS = 4
HEAD_DIM = 64
V_DIM = 2 * HEAD_DIM
D_CONV = 256
D_POOL = 256
D_ATTN = 512
CONV_WIDTH = 31
POOL_GROUP_DIM = 64
SUBLANES = 8
HALO = 16
V_ROWS = V_DIM + 8
BF16_EXACT_INT = 256
VMEM_LIMIT_BYTES = 56 * 1024 * 1024
NEG_BIG = -1e30
MAX_UNSHIFTED_SCORE = 60.0
UNSHIFTED_CHUNK = 16


def _attn_block(seq):
    return min(512, seq)


def _row_block(rows):
    return min(512, rows)


def _rms(x, g):
    ms = jnp.mean(x * x, axis=-1, keepdims=True)
    return x * lax.rsqrt(ms + NORM_EPS) * g


def _sigmoid(x):
    return 1.0 / (1.0 + jnp.exp(-x))


def _row_halves(tm):
    half = tm // 2 if tm % (2 * SUBLANES) == 0 else tm
    return [slice(r, r + half) for r in range(0, tm, half)]


def _swiglu_halves(hs, wg_ref, wu_ref, wd_ref):
    hbs = [h.astype(BF16) for h in hs]
    gus = [(jnp.dot(hb, wg_ref[...], preferred_element_type=F32),
            jnp.dot(hb, wu_ref[...], preferred_element_type=F32)) for hb in hbs]
    acts = [(g * _sigmoid(g) * u).astype(BF16) for g, u in gus]
    return [jnp.dot(a, wd_ref[...], preferred_element_type=F32) for a in acts]


def _resident(shape):
    nd = len(shape)
    return pl.BlockSpec(shape, lambda *_: (0,) * nd, pipeline_mode=pl.Buffered(1))


def _slope(h):
    return 2.0 ** (-8.0 * (h + 1) / N_HEADS)


def _attention_operands(uq, uk, uv, rows, gq_ref, gk_ref, qT_ref, ka_ref, vT_ref):
    n = uq.shape[0]
    blk = qT_ref.shape[-1]
    lane = lax.broadcasted_iota(jnp.int32, (n, V_DIM), 1)
    low = lane < HEAD_DIM
    pos = lax.broadcasted_iota(jnp.int32, (n, V_DIM), 0) + rows.start
    one = jnp.ones((n, V_DIM), F32)
    zero = jnp.zeros((n, V_DIM), F32)
    row8 = lax.broadcasted_iota(jnp.int32, (V_ROWS - V_DIM, n), 0)
    ones_rows = jnp.where(row8 == 0, 1.0, 0.0).astype(vT_ref.dtype)
    digits = [(pos % BF16_EXACT_INT).astype(F32)]
    place = BF16_EXACT_INT
    while place < blk:
        digits.append((pos % (place * BF16_EXACT_INT) - pos % place).astype(F32))
        place *= BF16_EXACT_INT

    def norm_pair(x, g):
        x2 = x * x
        s_lo = jnp.sum(jnp.where(low, x2, 0.0), axis=-1, keepdims=True)
        s_hi = jnp.sum(jnp.where(low, 0.0, x2), axis=-1, keepdims=True)
        ms = jnp.where(low, s_lo, s_hi) * (1.0 / HEAD_DIM)
        return x * lax.rsqrt(ms + NORM_EPS) * g

    def aug(cols):
        a = zero
        for k, col in enumerate(cols):
            a = jnp.where(lane == HEAD_DIM + k, col, a)
        return a

    def other_map(x):
        return pltpu.roll(x, HEAD_DIM, 1)

    for h in range(N_HEADS):
        cols = slice(h * V_DIM, (h + 1) * V_DIM)
        qn = norm_pair(uq[:, cols], gq_ref[...]) * (HEAD_DIM ** -0.5)
        kn = norm_pair(uk[:, cols], gk_ref[...])
        coef = _slope(h)
        q_aug = aug([-coef * one] * len(digits) + digits)
        k_aug = aug(digits + [coef * one] * len(digits))
        for c in range(2):
            qc = qn if c == 0 else other_map(qn)
            kc = kn if c == 0 else other_map(kn)
            qT_ref[h, c, :, rows] = jnp.where(low, qc, q_aug).T.astype(qT_ref.dtype)
            ka_ref[h, c, rows, :] = jnp.where(low, kc, k_aug).astype(ka_ref.dtype)
        vT_ref[h, :V_DIM, rows] = uv[:, cols].T.astype(vT_ref.dtype)
        vT_ref[h, V_DIM:, rows] = ones_rows


def _ffn1_win_kernel(x_ref, g1_ref, wg_ref, wu_ref, wd_ref, gm_ref, win_ref, gq_ref, gk_ref,
                     x1_ref, ucp_ref, qT_ref, ka_ref, vT_ref):
    halves = _row_halves(x_ref.shape[0])
    xs = [x_ref[r, :] for r in halves]
    ys = _swiglu_halves([_rms(x, g1_ref[...]) for x in xs], wg_ref, wu_ref, wd_ref)
    x1s = [x + 0.5 * y for x, y in zip(xs, ys)]
    hms = [_rms(x1, gm_ref[...]).astype(BF16) for x1 in x1s]
    c0 = 2 * D_CONV + D_POOL
    u_attn = [jnp.dot(hm, win_ref[:, c0:], preferred_element_type=F32) for hm in hms]
    for r, u in zip(halves, u_attn):
        _attention_operands(u[:, :D_ATTN], u[:, D_ATTN:2 * D_ATTN], u[:, 2 * D_ATTN:], r,
                            gq_ref, gk_ref, qT_ref, ka_ref, vT_ref)
    for r, x1, hm in zip(halves, x1s, hms):
        x1_ref[r, :] = x1
        ucp_ref[r, :] = jnp.dot(hm, win_ref[:, :c0], preferred_element_type=F32)


def _ffn1_win(x, g1, wg, wu, wd, gm, win, gq, gk, seq):
    rows, d = x.shape
    f = wg.shape[1]
    tm = _row_block(rows)
    c0 = 2 * D_CONV + D_POOL
    nblk = seq // tm
    assert tm == _attn_block(seq) and rows % seq == 0
    row_spec = lambda w: pl.BlockSpec((tm, w), lambda i: (i, 0))
    return pl.pallas_call(
        _ffn1_win_kernel,
        grid=(rows // tm,),
        in_specs=[row_spec(d), _resident((1, d)), _resident((d, f)), _resident((d, f)),
                  _resident((f, d)), _resident((1, d)), _resident(win.shape),
                  _resident(gq.shape), _resident(gk.shape)],
        out_specs=[row_spec(d), row_spec(c0),
                   pl.BlockSpec((None, N_HEADS, 2, V_DIM, tm), lambda i: (i // nblk, 0, 0, 0, i % nblk)),
                   pl.BlockSpec((None, N_HEADS, 2, tm, V_DIM), lambda i: (i // nblk, 0, 0, i % nblk, 0)),
                   pl.BlockSpec((None, N_HEADS, None, V_ROWS, tm),
                                lambda i: (i // nblk, 0, i % nblk, 0, 0))],
        out_shape=[jax.ShapeDtypeStruct((rows, d), F32),
                   jax.ShapeDtypeStruct((rows, c0), F32),
                   jax.ShapeDtypeStruct((rows // seq, N_HEADS, 2, V_DIM, seq), BF16),
                   jax.ShapeDtypeStruct((rows // seq, N_HEADS, 2, seq, V_DIM), BF16),
                   jax.ShapeDtypeStruct((rows // seq, N_HEADS, nblk, V_ROWS, tm), BF16)],
        compiler_params=pltpu.CompilerParams(dimension_semantics=("arbitrary",),
                                             vmem_limit_bytes=VMEM_LIMIT_BYTES),
        name="ffn1_win",
    )(x, g1, wg, wu, wd, gm, win, gq, gk)


def _convpool_tile(t0, seq, main_ref, prev_ref, next_ref, params, scratch, out_ref):
    dw_ref, dwb_ref, lng_ref, lnb_ref, pw_ref, ps_ref = params
    zext, zrot, pext, a1, a2, a3 = scratch
    ts = main_ref.shape[0]
    has_prev = t0 > 0
    has_next = t0 + ts < seq

    def glu(u):
        return u[:, :D_CONV] * _sigmoid(u[:, D_CONV:2 * D_CONV])

    def assemble():
        um = main_ref[...]
        up = prev_ref[...]
        un = next_ref[...]
        zero_halo = jnp.zeros((HALO, D_CONV), F32)
        zext[0:HALO, :] = jnp.where(has_prev, glu(up), zero_halo)
        zext[HALO:HALO + ts, :] = glu(um)
        zext[HALO + ts:, :] = jnp.where(has_next, glu(un), zero_halo)
        pext[0:HALO, :] = jnp.where(has_prev, up[:, 2 * D_CONV:], zero_halo)
        pext[HALO:HALO + ts, :] = um[:, 2 * D_CONV:]
        pext[HALO + ts:, :] = jnp.where(has_next, un[:, 2 * D_CONV:], zero_halo)

    def conv():
        off0 = HALO - CONV_WIDTH // 2
        nz = ts + 2 * HALO - SUBLANES
        for r in range(1, SUBLANES):
            zrot[r - 1, 0:nz, :] = zext[r:r + nz, :]
        acc = jnp.zeros((ts, D_CONV), F32)
        for j in range(CONV_WIDTH):
            r, a = (off0 + j) % SUBLANES, (off0 + j) // SUBLANES * SUBLANES
            tap = zext[a:a + ts, :] if r == 0 else zrot[r - 1, a:a + ts, :]
            acc = acc + dw_ref[j:j + 1, :] * tap
        z = acc + dwb_ref[...]
        mu = jnp.mean(z, axis=-1, keepdims=True)
        zc = z - mu
        var = jnp.mean(zc * zc, axis=-1, keepdims=True)
        y = zc * lax.rsqrt(var + NORM_EPS) * lng_ref[...] + lnb_ref[...]
        out_ref[:, :D_CONV] = (y * _sigmoid(y)).astype(out_ref.dtype)

    def pool():
        n = ts + 2 * HALO
        a1[1:n, :] = pext[0:n - 1, :] + pext[1:n, :]
        a2[2:n - 1, :] = a1[1:n - 2, :] + a1[3:n, :]
        a3[4:n - 3, :] = a2[2:n - 5, :] + a2[6:n - 1, :]
        w2 = a1[HALO:HALO + ts, :]
        w4 = a2[HALO:HALO + ts, :]
        w8 = a3[HALO:HALO + ts, :]
        w16 = a3[HALO - 4:HALO - 4 + ts, :] + a3[HALO + 4:HALO + 4 + ts, :]
        grp = lax.broadcasted_iota(jnp.int32, (ts, D_POOL), 1) // POOL_GROUP_DIM
        win = jnp.where(grp == 0, w2, jnp.where(grp == 1, w4, jnp.where(grp == 2, w8, w16)))
        t = lax.broadcasted_iota(jnp.int32, (ts, D_POOL), 0) + t0
        half = jnp.left_shift(1, grp)
        cnt = jnp.minimum(t + half, seq) - jnp.maximum(t - half, 0)
        d = win / cnt.astype(F32) - pext[HALO:HALO + ts, :]
        yp = jnp.dot(d.astype(BF16), pw_ref[...], preferred_element_type=F32) * ps_ref[...]
        out_ref[:, D_CONV:] = yp.astype(out_ref.dtype)

    return [assemble, conv, pool]


def _convpool_kernel(seq, main_ref, prev_ref, next_ref, dw_ref, dwb_ref, lng_ref, lnb_ref,
                     pw_ref, ps_ref, out_ref, *scratch):
    t0 = pl.program_id(1) * main_ref.shape[0]
    for stage in _convpool_tile(t0, seq, main_ref, prev_ref, next_ref,
                                (dw_ref, dwb_ref, lng_ref, lnb_ref, pw_ref, ps_ref), scratch, out_ref):
        stage()


def _convpool(ucp, dw, dwb, lng, lnb, pw_bd, ps):
    b, seq, c0 = ucp.shape
    ts = _attn_block(seq)
    nh = ts // HALO
    last_halo = seq // HALO - 1
    n = ts + 2 * HALO
    return pl.pallas_call(
        functools.partial(_convpool_kernel, seq),
        grid=(b, seq // ts),
        in_specs=[pl.BlockSpec((None, ts, c0), lambda bi, i: (bi, i, 0)),
                  pl.BlockSpec((None, HALO, c0), lambda bi, i: (bi, jnp.maximum(i * nh - 1, 0), 0)),
                  pl.BlockSpec((None, HALO, c0),
                               lambda bi, i: (bi, jnp.minimum((i + 1) * nh, last_halo), 0)),
                  _resident(dw.shape), _resident(dwb.shape), _resident(lng.shape),
                  _resident(lnb.shape), _resident(pw_bd.shape), _resident(ps.shape)],
        out_specs=pl.BlockSpec((None, ts, D_CONV + D_POOL), lambda bi, i: (bi, i, 0)),
        out_shape=jax.ShapeDtypeStruct((b, seq, D_CONV + D_POOL), BF16),
        scratch_shapes=[pltpu.VMEM((n, D_CONV), F32), pltpu.VMEM((SUBLANES - 1, n, D_CONV), F32),
                        pltpu.VMEM((n, D_POOL), F32), pltpu.VMEM((n, D_POOL), F32),
                        pltpu.VMEM((n, D_POOL), F32), pltpu.VMEM((n, D_POOL), F32)],
        compiler_params=pltpu.CompilerParams(dimension_semantics=("arbitrary", "arbitrary"),
                                             vmem_limit_bytes=VMEM_LIMIT_BYTES),
        name="convpool",
    )(ucp, ucp, ucp, dw, dwb, lng, lnb, pw_bd, ps)


def _attn_kernel(lambda_init, unshifted, slopes_ref, qT_ref, ka_ref, vT_ref, lq1_ref, lk1_ref,
                 lq2_ref, lk2_ref, sub_ref, out_ref, qv_ref, m_ref, acc_ref):
    h = pl.program_id(1)
    i = pl.program_id(2)
    nk = ka_ref.shape[1]
    tk = ka_ref.shape[2]
    tq = qT_ref.shape[2]
    slope = slopes_ref[h]

    row = lax.broadcasted_iota(jnp.int32, (V_DIM, tq), 0)
    for c in range(2):
        q = qT_ref[c]
        qv_ref[0, c] = jnp.where(row < HEAD_DIM, q, -q)
        qv_ref[1, c] = jnp.where(row < HEAD_DIM, q, jnp.zeros_like(q))
        qv_ref[2, c] = q
    acc_ref[...] = jnp.zeros(acc_ref.shape, F32)

    def diag_unit():
        kk = lax.broadcasted_iota(jnp.int32, (tk, tq), 0)
        qq = lax.broadcasted_iota(jnp.int32, (tk, tq), 1)
        return i, 1, slope * jnp.abs(kk - qq).astype(F32)

    def off_diag_unit(t):
        j = jnp.where(t >= i, t + 1, t)
        dist = jnp.abs(j - i) * tk
        return j, jnp.where(j > i, 2, 0), slope * jnp.full((1, tq), dist, jnp.int32).astype(F32)

    def scores(unit, c):
        j, side, bias = unit
        return jnp.dot(ka_ref[c, j], qv_ref[side, c], preferred_element_type=F32) - bias

    def unshifted_chunk(units):
        chains = [(u, c) for u in units for c in range(2)]
        probs = [None] * len(chains)
        pv = [[], []]

        def score_stage(k):
            probs[k] = jnp.exp(scores(*chains[k])).astype(BF16)

        def pv_stage(k):
            (j, _, _), c = chains[k]
            pv[c].append(jnp.dot(vT_ref[j], probs[k], preferred_element_type=F32))

        score_stage(0)
        for k in range(1, len(chains)):
            score_stage(k)
            pv_stage(k - 1)
        pv_stage(len(chains) - 1)
        for c in range(2):
            acc_ref[c] += functools.reduce(lambda a, b: a + b, pv[c])

    def running_max_step(unit):
        for c in range(2):
            s = scores(unit, c)
            m_old = m_ref[c]
            m_new = jnp.maximum(m_old, jnp.max(s, axis=0, keepdims=True))
            p = jnp.exp(s - m_new).astype(BF16)
            alpha = jnp.exp(m_old - m_new)
            acc_ref[c] = acc_ref[c] * alpha + jnp.dot(vT_ref[unit[0]], p, preferred_element_type=F32)
            m_ref[c] = m_new

    if unshifted:
        chunk = min(UNSHIFTED_CHUNK, nk)
        unshifted_chunk([diag_unit()] + [off_diag_unit(t) for t in range(chunk - 1)])

        def body(g, carry):
            t0 = chunk - 1 + g * chunk
            unshifted_chunk([off_diag_unit(t0 + r) for r in range(chunk)])
            return carry

        lax.fori_loop(0, nk // chunk - 1, body, 0)
    else:
        m_ref[...] = jnp.full(m_ref.shape, NEG_BIG, F32)
        running_max_step(diag_unit())

        def body(t, carry):
            running_max_step(off_diag_unit(t))
            return carry

        lax.fori_loop(0, nk - 1, body, 0)

    lam = (jnp.exp(jnp.sum(lq1_ref[...] * lk1_ref[...], axis=-1, keepdims=True))
           - jnp.exp(jnp.sum(lq2_ref[...] * lk2_ref[...], axis=-1, keepdims=True))
           + lambda_init)
    a0 = acc_ref[0]
    a1 = acc_ref[1]
    o = a0[:V_DIM] / a0[V_DIM:V_DIM + 1] - lam * (a1[:V_DIM] / a1[V_DIM:V_DIM + 1])
    ms = jnp.mean(o * o, axis=0, keepdims=True)
    y = o * lax.rsqrt(ms + NORM_EPS) * sub_ref[...] * (1.0 - lambda_init)
    out_ref[...] = y.T.astype(out_ref.dtype)


def _attention(qT, ka, vT, lq1, lk1, lq2, lk2, sub_col, lambda_init, unshifted):
    b, nh, _, kw, seq = qT.shape
    blk = vT.shape[-1]
    nblk = seq // blk
    assert kw == V_DIM and nblk % min(UNSHIFTED_CHUNK, nblk) == 0
    ka = ka.reshape(b, nh, 2, nblk, blk, kw)
    slopes = jnp.asarray([_slope(h) for h in range(nh)], F32)
    return pl.pallas_call(
        functools.partial(_attn_kernel, lambda_init, unshifted),
        grid=(b, nh, nblk),
        in_specs=[pl.BlockSpec(memory_space=pltpu.SMEM),
                  pl.BlockSpec((None, None, 2, kw, blk), lambda bi, h, i: (bi, h, 0, 0, i)),
                  pl.BlockSpec((None, None, 2, nblk, blk, kw), lambda bi, h, i: (bi, h, 0, 0, 0, 0)),
                  pl.BlockSpec((None, None, nblk, V_ROWS, blk), lambda bi, h, i: (bi, h, 0, 0, 0)),
                  _resident(lq1.shape), _resident(lk1.shape), _resident(lq2.shape),
                  _resident(lk2.shape), _resident(sub_col.shape)],
        out_specs=pl.BlockSpec((None, blk, V_DIM), lambda bi, h, i: (bi, i, h)),
        out_shape=jax.ShapeDtypeStruct((b, seq, nh * V_DIM), BF16),
        scratch_shapes=[pltpu.VMEM((3, 2, kw, blk), qT.dtype),
                        pltpu.VMEM((2, 1, blk), F32),
                        pltpu.VMEM((2, V_ROWS, blk), F32)],
        compiler_params=pltpu.CompilerParams(
            dimension_semantics=("arbitrary", "arbitrary", "arbitrary"),
            vmem_limit_bytes=VMEM_LIMIT_BYTES),
        name="attention_unshifted" if unshifted else "attention_running_max",
    )(slopes, qT, ka, vT, lq1, lk1, lq2, lk2, sub_col)


def _diff_attention(qT, ka, vT, q_gain, k_gain, lq1, lk1, lq2, lk2, sub_col, lambda_init):
    score_bound = (1.01 * HEAD_DIM ** 0.5) * jnp.max(jnp.abs(q_gain)) * jnp.max(jnp.abs(k_gain))

    def branch(unshifted):
        return lambda qT, ka, vT: _attention(qT, ka, vT, lq1, lk1, lq2, lk2, sub_col, lambda_init,
                                             unshifted)

    return lax.cond(score_bound <= MAX_UNSHIFTED_SCORE, branch(True), branch(False), qT, ka, vT)


def _out_ffn2_kernel(x1_ref, ycp_ref, yat_ref, wo_cp_ref, wo_at_ref, g2_ref, wg_ref, wu_ref,
                     wd_ref, gp_ref, out_ref):
    halves = _row_halves(x1_ref.shape[0])
    x2s = [x1_ref[r, :]
           + jnp.dot(ycp_ref[r, :], wo_cp_ref[...], preferred_element_type=F32)
           + jnp.dot(yat_ref[r, :], wo_at_ref[...], preferred_element_type=F32) for r in halves]
    ys = _swiglu_halves([_rms(x2, g2_ref[...]) for x2 in x2s], wg_ref, wu_ref, wd_ref)
    for r, x2, y in zip(halves, x2s, ys):
        out_ref[r, :] = _rms(x2 + 0.5 * y, gp_ref[...])


def _out_ffn2(x1, ycp, yat, wo_cp, wo_at, g2, wg, wu, wd, gp):
    rows, d = x1.shape
    f = wg.shape[1]
    tm = _row_block(rows)
    row_spec = lambda w: pl.BlockSpec((tm, w), lambda i: (i, 0))
    return pl.pallas_call(
        _out_ffn2_kernel,
        grid=(rows // tm,),
        in_specs=[row_spec(d), row_spec(ycp.shape[1]), row_spec(yat.shape[1]),
                  _resident(wo_cp.shape), _resident(wo_at.shape), _resident((1, d)),
                  _resident((d, f)), _resident((d, f)), _resident((f, d)), _resident((1, d))],
        out_specs=row_spec(d),
        out_shape=jax.ShapeDtypeStruct((rows, d), F32),
        compiler_params=pltpu.CompilerParams(dimension_semantics=("arbitrary",),
                                             vmem_limit_bytes=VMEM_LIMIT_BYTES),
        name="out_ffn2",
    )(x1, ycp, yat, wo_cp, wo_at, g2, wg, wu, wd, gp)


def _pool_block_diag(pool_w):
    g, cg, _ = pool_w.shape
    out = jnp.zeros((g * cg, g * cg), pool_w.dtype)
    for k in range(g):
        out = out.at[k * cg:(k + 1) * cg, k * cg:(k + 1) * cg].set(pool_w[k])
    return out


def kernel(x, ffn1_norm, ffn1_w_gate, ffn1_w_up, ffn1_w_down, mix_norm, w_in, conv_dw, conv_dw_bias, conv_ln_gain, conv_ln_bias, pool_w, pool_scale, q_norm, k_norm, lambda_q1, lambda_k1, lambda_q2, lambda_k2, attn_subln, w_out, ffn2_norm, ffn2_w_gate, ffn2_w_up, ffn2_w_down, post_norm):
    b, seq, d = x.shape
    depth = w_in.shape[0]
    rows = b * seq
    d_cp = D_CONV + D_POOL
    row = lambda v: v.reshape(1, -1).astype(F32)
    xf = x.reshape(rows, d).astype(F32)
    for l in range(depth):
        lambda_init = 0.8 - 0.6 * math.exp(-0.3 * l)
        pair = lambda g: jnp.concatenate([g, g]).reshape(1, -1).astype(F32)
        x1, ucp, qT, ka, vT = _ffn1_win(
            xf, row(ffn1_norm[l]), ffn1_w_gate[l].astype(BF16), ffn1_w_up[l].astype(BF16),
            ffn1_w_down[l].astype(BF16), row(mix_norm[l]), w_in[l].astype(BF16),
            pair(q_norm[l]), pair(k_norm[l]), seq)
        ycp = _convpool(
            ucp.reshape(b, seq, -1), conv_dw[l].astype(F32), row(conv_dw_bias[l]),
            row(conv_ln_gain[l]), row(conv_ln_bias[l]),
            _pool_block_diag(pool_w[l]).astype(BF16), row(pool_scale[l]))
        yat = _diff_attention(
            qT, ka, vT, q_norm[l], k_norm[l], row(lambda_q1[l]), row(lambda_k1[l]), row(lambda_q2[l]),
            row(lambda_k2[l]), attn_subln[l].reshape(-1, 1).astype(F32), lambda_init)
        wo = w_out[l].astype(BF16)
        xf = _out_ffn2(x1, ycp.reshape(rows, d_cp), yat.reshape(rows, -1), wo[:d_cp], wo[d_cp:],
                       row(ffn2_norm[l]), ffn2_w_gate[l].astype(BF16), ffn2_w_up[l].astype(BF16),
                       ffn2_w_down[l].astype(BF16), row(post_norm[l]))
    return xf.reshape(b, seq, d).astype(x.dtype)
```

```python
import functools
import math

import jax
import jax.numpy as jnp
from jax import lax
from jax.experimental import pallas as pl
from jax.experimental.pallas import tpu as pltpu

F32 = jnp.float32
BF16 = jnp.bfloat16

NORM_EPS = 1e-6
N_HEADS = 4
HEAD_DIM = 64
V_DIM = 2 * HEAD_DIM
D_CONV = 256
D_POOL = 256
D_ATTN = 512
CONV_WIDTH = 31
POOL_GROUP_DIM = 64
SUBLANES = 8
HALO = 16
V_ROWS = V_DIM + 8
V_STORE_ROWS = V_DIM + 16
BF16_EXACT_INT = 256
VMEM_LIMIT_BYTES = 56 * 1024 * 1024
NEG_BIG = -1e30
MAX_UNSHIFTED_SCORE = 60.0
UNSHIFTED_CHUNK = 16


def _attn_block(seq):
    return min(512, seq)


def _row_block(rows):
    return min(512, rows)


def _rms(x, g):
    ms = jnp.mean(x * x, axis=-1, keepdims=True)
    return x * lax.rsqrt(ms + NORM_EPS) * g


def _sigmoid(x):
    return 1.0 / (1.0 + jnp.exp(-x))


def _row_halves(tm):
    half = tm // 2 if tm % (2 * SUBLANES) == 0 else tm
    return [slice(r, r + half) for r in range(0, tm, half)]


def _swiglu_halves(hs, wg_ref, wu_ref, wd_ref):
    hbs = [h.astype(BF16) for h in hs]
    gus = [(jnp.dot(hb, wg_ref[...], preferred_element_type=F32),
            jnp.dot(hb, wu_ref[...], preferred_element_type=F32)) for hb in hbs]
    acts = [(g * _sigmoid(g) * u).astype(BF16) for g, u in gus]
    return [jnp.dot(a, wd_ref[...], preferred_element_type=F32) for a in acts]


def _resident(shape):
    nd = len(shape)
    return pl.BlockSpec(shape, lambda *_: (0,) * nd, pipeline_mode=pl.Buffered(1))


def _slope(h):
    return 2.0 ** (-8.0 * (h + 1) / N_HEADS)


def _attention_operands(uq, uk, uv, rows, gq_ref, gk_ref, qT_ref, ka_ref, vT_ref):
    n = uq.shape[0]
    blk = qT_ref.shape[-1]
    lane = lax.broadcasted_iota(jnp.int32, (n, V_DIM), 1)
    low = lane < HEAD_DIM
    pos = lax.broadcasted_iota(jnp.int32, (n, V_DIM), 0) + rows.start
    one = jnp.ones((n, V_DIM), F32)
    zero = jnp.zeros((n, V_DIM), F32)
    row8 = lax.broadcasted_iota(jnp.int32, (V_STORE_ROWS - V_DIM, n), 0)
    ones_rows = jnp.where(row8 == 0, 1.0, 0.0).astype(vT_ref.dtype)
    digits = [(pos % BF16_EXACT_INT).astype(F32)]
    place = BF16_EXACT_INT
    while place < blk:
        digits.append((pos % (place * BF16_EXACT_INT) - pos % place).astype(F32))
        place *= BF16_EXACT_INT

    def norm_pair(x, g):
        x2 = x * x
        s_lo = jnp.sum(jnp.where(low, x2, 0.0), axis=-1, keepdims=True)
        s_hi = jnp.sum(jnp.where(low, 0.0, x2), axis=-1, keepdims=True)
        ms = jnp.where(low, s_lo, s_hi) * (1.0 / HEAD_DIM)
        return x * lax.rsqrt(ms + NORM_EPS) * g

    def aug(cols):
        a = zero
        for k, col in enumerate(cols):
            a = jnp.where(lane == HEAD_DIM + k, col, a)
        return a

    def other_map(x):
        return pltpu.roll(x, HEAD_DIM, 1)

    for h in range(N_HEADS):
        cols = slice(h * V_DIM, (h + 1) * V_DIM)
        qn = norm_pair(uq[:, cols], gq_ref[...]) * (HEAD_DIM ** -0.5)
        kn = norm_pair(uk[:, cols], gk_ref[...])
        coef = _slope(h)
        q_aug = aug([-coef * one] * len(digits) + digits)
        k_aug = aug(digits + [coef * one] * len(digits))
        for c in range(2):
            qc = qn if c == 0 else other_map(qn)
            kc = kn if c == 0 else other_map(kn)
            qT_ref[h, c, :, rows] = jnp.where(low, qc, q_aug).T.astype(qT_ref.dtype)
            ka_ref[h, c, rows, :] = jnp.where(low, kc, k_aug).astype(ka_ref.dtype)
        vT_ref[h, :V_DIM, rows] = uv[:, cols].T.astype(vT_ref.dtype)
        vT_ref[h, V_DIM:, rows] = ones_rows


def _ffn1_win_kernel(x_ref, g1_ref, wg_ref, wu_ref, wd_ref, gm_ref, win_ref, gq_ref, gk_ref,
                     x1_ref, ucp_ref, qT_ref, ka_ref, vT_ref):
    halves = _row_halves(x_ref.shape[0])
    xs = [x_ref[r, :] for r in halves]
    ys = _swiglu_halves([_rms(x, g1_ref[...]) for x in xs], wg_ref, wu_ref, wd_ref)
    x1s = [x + 0.5 * y for x, y in zip(xs, ys)]
    hms = [_rms(x1, gm_ref[...]).astype(BF16) for x1 in x1s]
    c0 = 2 * D_CONV + D_POOL
    u_attn = [jnp.dot(hm, win_ref[:, c0:], preferred_element_type=F32) for hm in hms]
    for r, u in zip(halves, u_attn):
        _attention_operands(u[:, :D_ATTN], u[:, D_ATTN:2 * D_ATTN], u[:, 2 * D_ATTN:], r,
                            gq_ref, gk_ref, qT_ref, ka_ref, vT_ref)
    for r, x1, hm in zip(halves, x1s, hms):
        x1_ref[r, :] = x1
        ucp_ref[r, :] = jnp.dot(hm, win_ref[:, :c0], preferred_element_type=F32)


def _ffn1_win(x, g1, wg, wu, wd, gm, win, gq, gk, seq):
    rows, d = x.shape
    f = wg.shape[1]
    tm = _row_block(rows)
    c0 = 2 * D_CONV + D_POOL
    nblk = seq // tm
    assert tm == _attn_block(seq) and rows % seq == 0
    row_spec = lambda w: pl.BlockSpec((tm, w), lambda i: (i, 0))
    return pl.pallas_call(
        _ffn1_win_kernel,
        grid=(rows // tm,),
        in_specs=[row_spec(d), _resident((1, d)), _resident((d, f)), _resident((d, f)),
                  _resident((f, d)), _resident((1, d)), _resident(win.shape),
                  _resident(gq.shape), _resident(gk.shape)],
        out_specs=[row_spec(d), row_spec(c0),
                   pl.BlockSpec((None, N_HEADS, 2, V_DIM, tm), lambda i: (i // nblk, 0, 0, 0, i % nblk)),
                   pl.BlockSpec((None, N_HEADS, 2, tm, V_DIM), lambda i: (i // nblk, 0, 0, i % nblk, 0)),
                   pl.BlockSpec((None, N_HEADS, None, V_STORE_ROWS, tm),
                                lambda i: (i // nblk, 0, i % nblk, 0, 0))],
        out_shape=[jax.ShapeDtypeStruct((rows, d), F32),
                   jax.ShapeDtypeStruct((rows, c0), F32),
                   jax.ShapeDtypeStruct((rows // seq, N_HEADS, 2, V_DIM, seq), BF16),
                   jax.ShapeDtypeStruct((rows // seq, N_HEADS, 2, seq, V_DIM), BF16),
                   jax.ShapeDtypeStruct((rows // seq, N_HEADS, nblk, V_STORE_ROWS, tm), BF16)],
        compiler_params=pltpu.CompilerParams(dimension_semantics=("arbitrary",),
                                             vmem_limit_bytes=VMEM_LIMIT_BYTES),
        name="ffn1_win",
    )(x, g1, wg, wu, wd, gm, win, gq, gk)


def _convpool_tile(t0, seq, main_ref, prev_ref, next_ref, params, scratch, out_ref):
    dw_ref, dwb_ref, lng_ref, lnb_ref, pw_ref, ps_ref = params
    zext, zrot, pext, a1, a2, a3 = scratch
    ts = main_ref.shape[0]
    has_prev = t0 > 0
    has_next = t0 + ts < seq

    def glu(u):
        return u[:, :D_CONV] * _sigmoid(u[:, D_CONV:2 * D_CONV])

    def assemble():
        um = main_ref[...]
        up = prev_ref[...]
        un = next_ref[...]
        zero_halo = jnp.zeros((HALO, D_CONV), F32)
        zext[0:HALO, :] = jnp.where(has_prev, glu(up), zero_halo)
        zext[HALO:HALO + ts, :] = glu(um)
        zext[HALO + ts:, :] = jnp.where(has_next, glu(un), zero_halo)
        pext[0:HALO, :] = jnp.where(has_prev, up[:, 2 * D_CONV:], zero_halo)
        pext[HALO:HALO + ts, :] = um[:, 2 * D_CONV:]
        pext[HALO + ts:, :] = jnp.where(has_next, un[:, 2 * D_CONV:], zero_halo)

    def conv():
        off0 = HALO - CONV_WIDTH // 2
        nz = ts + 2 * HALO - SUBLANES
        for r in range(1, SUBLANES):
            zrot[r - 1, 0:nz, :] = zext[r:r + nz, :]
        acc = jnp.zeros((ts, D_CONV), F32)
        for j in range(CONV_WIDTH):
            r, a = (off0 + j) % SUBLANES, (off0 + j) // SUBLANES * SUBLANES
            tap = zext[a:a + ts, :] if r == 0 else zrot[r - 1, a:a + ts, :]
            acc = acc + dw_ref[j:j + 1, :] * tap
        z = acc + dwb_ref[...]
        mu = jnp.mean(z, axis=-1, keepdims=True)
        zc = z - mu
        var = jnp.mean(zc * zc, axis=-1, keepdims=True)
        y = zc * lax.rsqrt(var + NORM_EPS) * lng_ref[...] + lnb_ref[...]
        out_ref[:, :D_CONV] = (y * _sigmoid(y)).astype(out_ref.dtype)

    def pool():
        n = ts + 2 * HALO
        a1[1:n, :] = pext[0:n - 1, :] + pext[1:n, :]
        a2[2:n - 1, :] = a1[1:n - 2, :] + a1[3:n, :]
        a3[4:n - 3, :] = a2[2:n - 5, :] + a2[6:n - 1, :]
        w2 = a1[HALO:HALO + ts, :]
        w4 = a2[HALO:HALO + ts, :]
        w8 = a3[HALO:HALO + ts, :]
        w16 = a3[HALO - 4:HALO - 4 + ts, :] + a3[HALO + 4:HALO + 4 + ts, :]
        grp = lax.broadcasted_iota(jnp.int32, (ts, D_POOL), 1) // POOL_GROUP_DIM
        win = jnp.where(grp == 0, w2, jnp.where(grp == 1, w4, jnp.where(grp == 2, w8, w16)))
        t = lax.broadcasted_iota(jnp.int32, (ts, D_POOL), 0) + t0
        half = jnp.left_shift(1, grp)
        cnt = jnp.minimum(t + half, seq) - jnp.maximum(t - half, 0)
        d = win / cnt.astype(F32) - pext[HALO:HALO + ts, :]
        yp = jnp.dot(d.astype(BF16), pw_ref[...], preferred_element_type=F32) * ps_ref[...]
        out_ref[:, D_CONV:] = yp.astype(out_ref.dtype)

    return [assemble, conv, pool]


def _convpool_kernel(seq, main_ref, prev_ref, next_ref, dw_ref, dwb_ref, lng_ref, lnb_ref,
                     pw_ref, ps_ref, out_ref, *scratch):
    t0 = pl.program_id(1) * main_ref.shape[0]
    for stage in _convpool_tile(t0, seq, main_ref, prev_ref, next_ref,
                                (dw_ref, dwb_ref, lng_ref, lnb_ref, pw_ref, ps_ref), scratch, out_ref):
        stage()


def _convpool(ucp, dw, dwb, lng, lnb, pw_bd, ps):
    b, seq, c0 = ucp.shape
    ts = _attn_block(seq)
    nh = ts // HALO
    last_halo = seq // HALO - 1
    n = ts + 2 * HALO
    return pl.pallas_call(
        functools.partial(_convpool_kernel, seq),
        grid=(b, seq // ts),
        in_specs=[pl.BlockSpec((None, ts, c0), lambda bi, i: (bi, i, 0)),
                  pl.BlockSpec((None, HALO, c0), lambda bi, i: (bi, jnp.maximum(i * nh - 1, 0), 0)),
                  pl.BlockSpec((None, HALO, c0),
                               lambda bi, i: (bi, jnp.minimum((i + 1) * nh, last_halo), 0)),
                  _resident(dw.shape), _resident(dwb.shape), _resident(lng.shape),
                  _resident(lnb.shape), _resident(pw_bd.shape), _resident(ps.shape)],
        out_specs=pl.BlockSpec((None, ts, D_CONV + D_POOL), lambda bi, i: (bi, i, 0)),
        out_shape=jax.ShapeDtypeStruct((b, seq, D_CONV + D_POOL), BF16),
        scratch_shapes=[pltpu.VMEM((n, D_CONV), F32), pltpu.VMEM((SUBLANES - 1, n, D_CONV), F32),
                        pltpu.VMEM((n, D_POOL), F32), pltpu.VMEM((n, D_POOL), F32),
                        pltpu.VMEM((n, D_POOL), F32), pltpu.VMEM((n, D_POOL), F32)],
        compiler_params=pltpu.CompilerParams(dimension_semantics=("arbitrary", "arbitrary"),
                                             vmem_limit_bytes=VMEM_LIMIT_BYTES),
        name="convpool",
    )(ucp, ucp, ucp, dw, dwb, lng, lnb, pw_bd, ps)


def _attn_kernel(lambda_init, unshifted, slopes_ref, qT_ref, ka_ref, vT_ref, lq1_ref, lk1_ref,
                 lq2_ref, lk2_ref, sub_ref, out_ref, qv_ref, m_ref, acc_ref):
    h = pl.program_id(1)
    i = pl.program_id(2)
    nk = ka_ref.shape[1]
    tk = ka_ref.shape[2]
    tq = qT_ref.shape[2]
    slope = slopes_ref[h]

    row = lax.broadcasted_iota(jnp.int32, (V_DIM, tq), 0)
    for c in range(2):
        q = qT_ref[c]
        qv_ref[0, c] = jnp.where(row < HEAD_DIM, q, -q)
        qv_ref[1, c] = jnp.where(row < HEAD_DIM, q, jnp.zeros_like(q))
        qv_ref[2, c] = q
    acc_ref[...] = jnp.zeros(acc_ref.shape, F32)

    def diag_unit():
        kk = lax.broadcasted_iota(jnp.int32, (tk, tq), 0)
        qq = lax.broadcasted_iota(jnp.int32, (tk, tq), 1)
        return i, 1, slope * jnp.abs(kk - qq).astype(F32)

    def off_diag_unit(t):
        j = jnp.where(t >= i, t + 1, t)
        dist = jnp.abs(j - i) * tk
        return j, jnp.where(j > i, 2, 0), slope * jnp.full((1, tq), dist, jnp.int32).astype(F32)

    def scores(unit, c):
        j, side, bias = unit
        return jnp.dot(ka_ref[c, j], qv_ref[side, c], preferred_element_type=F32) - bias

    def unshifted_chunk(units):
        chains = [(u, c) for u in units for c in range(2)]
        probs = [None] * len(chains)
        pv = [[], []]

        def score_stage(k):
            probs[k] = jnp.exp(scores(*chains[k])).astype(BF16)

        def pv_stage(k):
            (j, _, _), c = chains[k]
            pv[c].append(jnp.dot(vT_ref[j, :V_ROWS, :], probs[k], preferred_element_type=F32))

        score_stage(0)
        for k in range(1, len(chains)):
            score_stage(k)
            pv_stage(k - 1)
        pv_stage(len(chains) - 1)
        for c in range(2):
            acc_ref[c] += functools.reduce(lambda a, b: a + b, pv[c])

    def running_max_step(unit):
        for c in range(2):
            s = scores(unit, c)
            m_old = m_ref[c]
            m_new = jnp.maximum(m_old, jnp.max(s, axis=0, keepdims=True))
            p = jnp.exp(s - m_new).astype(BF16)
            alpha = jnp.exp(m_old - m_new)
            acc_ref[c] = acc_ref[c] * alpha + jnp.dot(vT_ref[unit[0], :V_ROWS, :], p,
                                                      preferred_element_type=F32)
            m_ref[c] = m_new

    if unshifted:
        chunk = min(UNSHIFTED_CHUNK, nk)
        unshifted_chunk([diag_unit()] + [off_diag_unit(t) for t in range(chunk - 1)])

        def body(g, carry):
            t0 = chunk - 1 + g * chunk
            unshifted_chunk([off_diag_unit(t0 + r) for r in range(chunk)])
            return carry

        lax.fori_loop(0, nk // chunk - 1, body, 0)
    else:
        m_ref[...] = jnp.full(m_ref.shape, NEG_BIG, F32)
        running_max_step(diag_unit())

        def body(t, carry):
            running_max_step(off_diag_unit(t))
            return carry

        lax.fori_loop(0, nk - 1, body, 0)

    lam = (jnp.exp(jnp.sum(lq1_ref[...] * lk1_ref[...], axis=-1, keepdims=True))
           - jnp.exp(jnp.sum(lq2_ref[...] * lk2_ref[...], axis=-1, keepdims=True))
           + lambda_init)
    a0 = acc_ref[0]
    a1 = acc_ref[1]
    o = a0[:V_DIM] / a0[V_DIM:V_DIM + 1] - lam * (a1[:V_DIM] / a1[V_DIM:V_DIM + 1])
    ms = jnp.mean(o * o, axis=0, keepdims=True)
    y = o * lax.rsqrt(ms + NORM_EPS) * sub_ref[...] * (1.0 - lambda_init)
    out_ref[...] = y.T.astype(out_ref.dtype)


def _attention(qT, ka, vT, lq1, lk1, lq2, lk2, sub_col, lambda_init, unshifted):
    b, nh, _, kw, seq = qT.shape
    blk = vT.shape[-1]
    nblk = seq // blk
    assert kw == V_DIM and nblk % min(UNSHIFTED_CHUNK, nblk) == 0
    ka = ka.reshape(b, nh, 2, nblk, blk, kw)
    slopes = jnp.asarray([_slope(h) for h in range(nh)], F32)
    return pl.pallas_call(
        functools.partial(_attn_kernel, lambda_init, unshifted),
        grid=(b, nh, nblk),
        in_specs=[pl.BlockSpec(memory_space=pltpu.SMEM),
                  pl.BlockSpec((None, None, 2, kw, blk), lambda bi, h, i: (bi, h, 0, 0, i)),
                  pl.BlockSpec((None, None, 2, nblk, blk, kw), lambda bi, h, i: (bi, h, 0, 0, 0, 0)),
                  pl.BlockSpec((None, None, nblk, V_STORE_ROWS, blk), lambda bi, h, i: (bi, h, 0, 0, 0)),
                  _resident(lq1.shape), _resident(lk1.shape), _resident(lq2.shape),
                  _resident(lk2.shape), _resident(sub_col.shape)],
        out_specs=pl.BlockSpec((None, blk, V_DIM), lambda bi, h, i: (bi, i, h)),
        out_shape=jax.ShapeDtypeStruct((b, seq, nh * V_DIM), BF16),
        scratch_shapes=[pltpu.VMEM((3, 2, kw, blk), qT.dtype),
                        pltpu.VMEM((2, 1, blk), F32),
                        pltpu.VMEM((2, V_ROWS, blk), F32)],
        compiler_params=pltpu.CompilerParams(
            dimension_semantics=("arbitrary", "arbitrary", "arbitrary"),
            vmem_limit_bytes=VMEM_LIMIT_BYTES),
        name="attention_unshifted" if unshifted else "attention_running_max",
    )(slopes, qT, ka, vT, lq1, lk1, lq2, lk2, sub_col)


def _diff_attention(qT, ka, vT, q_gain, k_gain, lq1, lk1, lq2, lk2, sub_col, lambda_init):
    score_bound = (1.01 * HEAD_DIM ** 0.5) * jnp.max(jnp.abs(q_gain)) * jnp.max(jnp.abs(k_gain))

    def branch(unshifted):
        return lambda qT, ka, vT: _attention(qT, ka, vT, lq1, lk1, lq2, lk2, sub_col, lambda_init,
                                             unshifted)

    return lax.cond(score_bound <= MAX_UNSHIFTED_SCORE, branch(True), branch(False), qT, ka, vT)


def _out_ffn2_kernel(x1_ref, ycp_ref, yat_ref, wo_cp_ref, wo_at_ref, g2_ref, wg_ref, wu_ref,
                     wd_ref, gp_ref, out_ref):
    halves = _row_halves(x1_ref.shape[0])
    x2s = [x1_ref[r, :]
           + jnp.dot(ycp_ref[r, :], wo_cp_ref[...], preferred_element_type=F32)
           + jnp.dot(yat_ref[r, :], wo_at_ref[...], preferred_element_type=F32) for r in halves]
    ys = _swiglu_halves([_rms(x2, g2_ref[...]) for x2 in x2s], wg_ref, wu_ref, wd_ref)
    for r, x2, y in zip(halves, x2s, ys):
        out_ref[r, :] = _rms(x2 + 0.5 * y, gp_ref[...])


def _out_ffn2(x1, ycp, yat, wo_cp, wo_at, g2, wg, wu, wd, gp):
    rows, d = x1.shape
    f = wg.shape[1]
    tm = _row_block(rows)
    row_spec = lambda w: pl.BlockSpec((tm, w), lambda i: (i, 0))
    return pl.pallas_call(
        _out_ffn2_kernel,
        grid=(rows // tm,),
        in_specs=[row_spec(d), row_spec(ycp.shape[1]), row_spec(yat.shape[1]),
                  _resident(wo_cp.shape), _resident(wo_at.shape), _resident((1, d)),
                  _resident((d, f)), _resident((d, f)), _resident((f, d)), _resident((1, d))],
        out_specs=row_spec(d),
        out_shape=jax.ShapeDtypeStruct((rows, d), F32),
        compiler_params=pltpu.CompilerParams(dimension_semantics=("arbitrary",),
                                             vmem_limit_bytes=VMEM_LIMIT_BYTES),
        name="out_ffn2",
    )(x1, ycp, yat, wo_cp, wo_at, g2, wg, wu, wd, gp)


def _pool_block_diag(pool_w):
    g, cg, _ = pool_w.shape
    out = jnp.zeros((g * cg, g * cg), pool_w.dtype)
    for k in range(g):
        out = out.at[k * cg:(k + 1) * cg, k * cg:(k + 1) * cg].set(pool_w[k])
    return out


def kernel(x, ffn1_norm, ffn1_w_gate, ffn1_w_up, ffn1_w_down, mix_norm, w_in, conv_dw, conv_dw_bias, conv_ln_gain, conv_ln_bias, pool_w, pool_scale, q_norm, k_norm, lambda_q1, lambda_k1, lambda_q2, lambda_k2, attn_subln, w_out, ffn2_norm, ffn2_w_gate, ffn2_w_up, ffn2_w_down, post_norm):
    b, seq, d = x.shape
    depth = w_in.shape[0]
    rows = b * seq
    d_cp = D_CONV + D_POOL
    row = lambda v: v.reshape(1, -1).astype(F32)
    xf = x.reshape(rows, d).astype(F32)
    for l in range(depth):
        lambda_init = 0.8 - 0.6 * math.exp(-0.3 * l)
        pair = lambda g: jnp.concatenate([g, g]).reshape(1, -1).astype(F32)
        x1, ucp, qT, ka, vT = _ffn1_win(
            xf, row(ffn1_norm[l]), ffn1_w_gate[l].astype(BF16), ffn1_w_up[l].astype(BF16),
            ffn1_w_down[l].astype(BF16), row(mix_norm[l]), w_in[l].astype(BF16),
            pair(q_norm[l]), pair(k_norm[l]), seq)
        ycp = _convpool(
            ucp.reshape(b, seq, -1), conv_dw[l].astype(F32), row(conv_dw_bias[l]),
            row(conv_ln_gain[l]), row(conv_ln_bias[l]),
            _pool_block_diag(pool_w[l]).astype(BF16), row(pool_scale[l]))
        yat = _diff_attention(
            qT, ka, vT, q_norm[l], k_norm[l], row(lambda_q1[l]), row(lambda_k1[l]), row(lambda_q2[l]),
            row(lambda_k2[l]), attn_subln[l].reshape(-1, 1).astype(F32), lambda_init)
        wo = w_out[l].astype(BF16)
        xf = _out_ffn2(x1, ycp.reshape(rows, d_cp), yat.reshape(rows, -1), wo[:d_cp], wo[d_cp:],
                       row(ffn2_norm[l]), ffn2_w_gate[l].astype(BF16), ffn2_w_up[l].astype(BF16),
                       ffn2_w_down[l].astype(BF16), row(post_norm[l]))
    return xf.reshape(b, seq, d).astype(x.dtype)
```

```python
import functools
import math

import jax
import jax.numpy as jnp
from jax import lax
from jax.experimental import pallas as pl
from jax.experimental.pallas import tpu as pltpu

F32 = jnp.float32
BF16 = jnp.bfloat16

NORM_EPS = 1e-6
N_HEADS = 4
HEAD_DIM = 64
V_DIM = 2 * HEAD_DIM
D_CONV = 256
D_POOL = 256
D_ATTN = 512
CONV_WIDTH = 31
POOL_GROUP_DIM = 64
SUBLANES = 8
HALO = 16
V_ROWS = V_DIM + 8
BF16_EXACT_INT = 256
VMEM_LIMIT_BYTES = 56 * 1024 * 1024
NEG_BIG = -1e30
MAX_UNSHIFTED_SCORE = 60.0
UNSHIFTED_CHUNK = 16


def _attn_block(seq):
    return min(512, seq)


def _row_block(rows):
    return min(512, rows)


def _rms(x, g):
    ms = jnp.mean(x * x, axis=-1, keepdims=True)
    return x * lax.rsqrt(ms + NORM_EPS) * g


def _sigmoid(x):
    return 1.0 / (1.0 + jnp.exp(-x))


def _row_halves(tm):
    half = tm // 2 if tm % (2 * SUBLANES) == 0 else tm
    return [slice(r, r + half) for r in range(0, tm, half)]


def _swiglu_halves(hs, wg_ref, wu_ref, wd_ref):
    hbs = [h.astype(BF16) for h in hs]
    gus = [(jnp.dot(hb, wg_ref[...], preferred_element_type=F32),
            jnp.dot(hb, wu_ref[...], preferred_element_type=F32)) for hb in hbs]
    acts = [(g * _sigmoid(g) * u).astype(BF16) for g, u in gus]
    return [jnp.dot(a, wd_ref[...], preferred_element_type=F32) for a in acts]


def _resident(shape):
    nd = len(shape)
    return pl.BlockSpec(shape, lambda *_: (0,) * nd, pipeline_mode=pl.Buffered(1))


def _slope(h):
    return 2.0 ** (-8.0 * (h + 1) / N_HEADS)


def _attention_operands(uq, uk, uv, rows, gq_ref, gk_ref, qT_ref, ka_ref, vT_ref):
    n = uq.shape[0]
    blk = qT_ref.shape[-1]
    lane = lax.broadcasted_iota(jnp.int32, (n, V_DIM), 1)
    low = lane < HEAD_DIM
    pos = lax.broadcasted_iota(jnp.int32, (n, V_DIM), 0) + rows.start
    one = jnp.ones((n, V_DIM), F32)
    zero = jnp.zeros((n, V_DIM), F32)
    row8 = lax.broadcasted_iota(jnp.int32, (V_ROWS - V_DIM, n), 0)
    ones_rows = jnp.where(row8 == 0, 1.0, 0.0).astype(vT_ref.dtype)
    digits = [(pos % BF16_EXACT_INT).astype(F32)]
    place = BF16_EXACT_INT
    while place < blk:
        digits.append((pos % (place * BF16_EXACT_INT) - pos % place).astype(F32))
        place *= BF16_EXACT_INT

    def norm_pair(x, g):
        x2 = x * x
        s_lo = jnp.sum(jnp.where(low, x2, 0.0), axis=-1, keepdims=True)
        s_hi = jnp.sum(jnp.where(low, 0.0, x2), axis=-1, keepdims=True)
        ms = jnp.where(low, s_lo, s_hi) * (1.0 / HEAD_DIM)
        return x * lax.rsqrt(ms + NORM_EPS) * g

    def aug(cols):
        a = zero
        for k, col in enumerate(cols):
            a = jnp.where(lane == HEAD_DIM + k, col, a)
        return a

    def other_map(x):
        return pltpu.roll(x, HEAD_DIM, 1)

    for h in range(N_HEADS):
        cols = slice(h * V_DIM, (h + 1) * V_DIM)
        qn = norm_pair(uq[:, cols], gq_ref[...]) * (HEAD_DIM ** -0.5)
        kn = norm_pair(uk[:, cols], gk_ref[...])
        coef = _slope(h)
        q_aug = aug([-coef * one] * len(digits) + digits)
        k_aug = aug(digits + [coef * one] * len(digits))
        for c in range(2):
            qc = qn if c == 0 else other_map(qn)
            kc = kn if c == 0 else other_map(kn)
            qT_ref[h, c, :, rows] = jnp.where(low, qc, q_aug).T.astype(qT_ref.dtype)
            ka_ref[h, c, rows, :] = jnp.where(low, kc, k_aug).astype(ka_ref.dtype)
        vT_ref[h, :V_DIM, rows] = uv[:, cols].T.astype(vT_ref.dtype)
        vT_ref[h, V_DIM:, rows] = ones_rows


def _rider_specs(weights, n, steps):
    views, in_specs, out_specs, out_shapes = [], [], [], []
    rpb = n // steps
    assert n % steps == 0 and rpb % (2 * SUBLANES) == 0
    for w in weights:
        v = w.reshape(n, -1)
        views.append(v)
        in_specs.append(pl.BlockSpec((rpb, v.shape[1]), lambda i: (i, 0)))
        out_specs.append(pl.BlockSpec((rpb, v.shape[1]), lambda i: (i, 0)))
        out_shapes.append(jax.ShapeDtypeStruct(v.shape, BF16))
    return views, in_specs, out_specs, out_shapes


def _cast_riders(in_refs, out_refs):
    for src, dst in zip(in_refs, out_refs):
        dst[...] = src[...].astype(dst.dtype)


def _ffn1_win_kernel(n_riders, x_ref, g1_ref, wg_ref, wu_ref, wd_ref, gm_ref, win_ref, gq_ref, gk_ref,
                     *refs):
    x1_ref, ucp_ref, qT_ref, ka_ref, vT_ref = refs[n_riders:n_riders + 5]
    _cast_riders(refs[:n_riders], refs[n_riders + 5:])
    halves = _row_halves(x_ref.shape[0])
    xs = [x_ref[r, :] for r in halves]
    ys = _swiglu_halves([_rms(x, g1_ref[...]) for x in xs], wg_ref, wu_ref, wd_ref)
    x1s = [x + 0.5 * y for x, y in zip(xs, ys)]
    hms = [_rms(x1, gm_ref[...]).astype(BF16) for x1 in x1s]
    c0 = 2 * D_CONV + D_POOL
    u_attn = [jnp.dot(hm, win_ref[:, c0:], preferred_element_type=F32) for hm in hms]
    for r, u in zip(halves, u_attn):
        _attention_operands(u[:, :D_ATTN], u[:, D_ATTN:2 * D_ATTN], u[:, 2 * D_ATTN:], r,
                            gq_ref, gk_ref, qT_ref, ka_ref, vT_ref)
    for r, x1, hm in zip(halves, x1s, hms):
        x1_ref[r, :] = x1
        ucp_ref[r, :] = jnp.dot(hm, win_ref[:, :c0], preferred_element_type=F32)


def _ffn1_win(x, g1, wg, wu, wd, gm, win, gq, gk, seq, riders):
    rows, d = x.shape
    f = wg.shape[1]
    tm = _row_block(rows)
    c0 = 2 * D_CONV + D_POOL
    nblk = seq // tm
    assert tm == _attn_block(seq) and rows % seq == 0
    row_spec = lambda w: pl.BlockSpec((tm, w), lambda i: (i, 0))
    views, r_in, r_out, r_shapes = _rider_specs(riders, d, rows // tm)
    outs = pl.pallas_call(
        functools.partial(_ffn1_win_kernel, len(riders)),
        grid=(rows // tm,),
        in_specs=[row_spec(d), _resident((1, d)), _resident((d, f)), _resident((d, f)),
                  _resident((f, d)), _resident((1, d)), _resident(win.shape),
                  _resident(gq.shape), _resident(gk.shape)] + r_in,
        out_specs=[row_spec(d), row_spec(c0),
                   pl.BlockSpec((None, N_HEADS, 2, V_DIM, tm), lambda i: (i // nblk, 0, 0, 0, i % nblk)),
                   pl.BlockSpec((None, N_HEADS, 2, tm, V_DIM), lambda i: (i // nblk, 0, 0, i % nblk, 0)),
                   pl.BlockSpec((None, N_HEADS, None, V_ROWS, tm),
                                lambda i: (i // nblk, 0, i % nblk, 0, 0))] + r_out,
        out_shape=[jax.ShapeDtypeStruct((rows, d), F32),
                   jax.ShapeDtypeStruct((rows, c0), F32),
                   jax.ShapeDtypeStruct((rows // seq, N_HEADS, 2, V_DIM, seq), BF16),
                   jax.ShapeDtypeStruct((rows // seq, N_HEADS, 2, seq, V_DIM), BF16),
                   jax.ShapeDtypeStruct((rows // seq, N_HEADS, nblk, V_ROWS, tm), BF16)] + r_shapes,
        compiler_params=pltpu.CompilerParams(dimension_semantics=("arbitrary",),
                                             vmem_limit_bytes=VMEM_LIMIT_BYTES),
        name="ffn1_win",
    )(x, g1, wg, wu, wd, gm, win, gq, gk, *views)
    return (*outs[:5], [o.reshape(w.shape) for o, w in zip(outs[5:], riders)])


def _convpool_tile(t0, seq, main_ref, prev_ref, next_ref, params, scratch, out_ref):
    dw_ref, dwb_ref, lng_ref, lnb_ref, pw_ref, ps_ref = params
    zext, zrot, pext, a1, a2, a3 = scratch
    ts = main_ref.shape[0]
    has_prev = t0 > 0
    has_next = t0 + ts < seq

    def glu(u):
        return u[:, :D_CONV] * _sigmoid(u[:, D_CONV:2 * D_CONV])

    def assemble():
        um = main_ref[...]
        up = prev_ref[...]
        un = next_ref[...]
        zero_halo = jnp.zeros((HALO, D_CONV), F32)
        zext[0:HALO, :] = jnp.where(has_prev, glu(up), zero_halo)
        zext[HALO:HALO + ts, :] = glu(um)
        zext[HALO + ts:, :] = jnp.where(has_next, glu(un), zero_halo)
        pext[0:HALO, :] = jnp.where(has_prev, up[:, 2 * D_CONV:], zero_halo)
        pext[HALO:HALO + ts, :] = um[:, 2 * D_CONV:]
        pext[HALO + ts:, :] = jnp.where(has_next, un[:, 2 * D_CONV:], zero_halo)

    def conv():
        off0 = HALO - CONV_WIDTH // 2
        nz = ts + 2 * HALO - SUBLANES
        for r in range(1, SUBLANES):
            zrot[r - 1, 0:nz, :] = zext[r:r + nz, :]
        acc = jnp.zeros((ts, D_CONV), F32)
        for j in range(CONV_WIDTH):
            r, a = (off0 + j) % SUBLANES, (off0 + j) // SUBLANES * SUBLANES
            tap = zext[a:a + ts, :] if r == 0 else zrot[r - 1, a:a + ts, :]
            acc = acc + dw_ref[j:j + 1, :] * tap
        z = acc + dwb_ref[...]
        mu = jnp.mean(z, axis=-1, keepdims=True)
        zc = z - mu
        var = jnp.mean(zc * zc, axis=-1, keepdims=True)
        y = zc * lax.rsqrt(var + NORM_EPS) * lng_ref[...] + lnb_ref[...]
        out_ref[:, :D_CONV] = (y * _sigmoid(y)).astype(out_ref.dtype)

    def pool():
        n = ts + 2 * HALO
        a1[1:n, :] = pext[0:n - 1, :] + pext[1:n, :]
        a2[2:n - 1, :] = a1[1:n - 2, :] + a1[3:n, :]
        a3[4:n - 3, :] = a2[2:n - 5, :] + a2[6:n - 1, :]
        w2 = a1[HALO:HALO + ts, :]
        w4 = a2[HALO:HALO + ts, :]
        w8 = a3[HALO:HALO + ts, :]
        w16 = a3[HALO - 4:HALO - 4 + ts, :] + a3[HALO + 4:HALO + 4 + ts, :]
        grp = lax.broadcasted_iota(jnp.int32, (ts, D_POOL), 1) // POOL_GROUP_DIM
        win = jnp.where(grp == 0, w2, jnp.where(grp == 1, w4, jnp.where(grp == 2, w8, w16)))
        t = lax.broadcasted_iota(jnp.int32, (ts, D_POOL), 0) + t0
        half = jnp.left_shift(1, grp)
        cnt = jnp.minimum(t + half, seq) - jnp.maximum(t - half, 0)
        d = win / cnt.astype(F32) - pext[HALO:HALO + ts, :]
        yp = jnp.dot(d.astype(BF16), pw_ref[...], preferred_element_type=F32) * ps_ref[...]
        out_ref[:, D_CONV:] = yp.astype(out_ref.dtype)

    return [assemble, conv, pool]


def _convpool_kernel(seq, main_ref, prev_ref, next_ref, dw_ref, dwb_ref, lng_ref, lnb_ref,
                     pw_ref, ps_ref, out_ref, *scratch):
    t0 = pl.program_id(1) * main_ref.shape[0]
    for stage in _convpool_tile(t0, seq, main_ref, prev_ref, next_ref,
                                (dw_ref, dwb_ref, lng_ref, lnb_ref, pw_ref, ps_ref), scratch, out_ref):
        stage()


def _convpool(ucp, dw, dwb, lng, lnb, pw_bd, ps):
    b, seq, c0 = ucp.shape
    ts = _attn_block(seq)
    nh = ts // HALO
    last_halo = seq // HALO - 1
    n = ts + 2 * HALO
    return pl.pallas_call(
        functools.partial(_convpool_kernel, seq),
        grid=(b, seq // ts),
        in_specs=[pl.BlockSpec((None, ts, c0), lambda bi, i: (bi, i, 0)),
                  pl.BlockSpec((None, HALO, c0), lambda bi, i: (bi, jnp.maximum(i * nh - 1, 0), 0)),
                  pl.BlockSpec((None, HALO, c0),
                               lambda bi, i: (bi, jnp.minimum((i + 1) * nh, last_halo), 0)),
                  _resident(dw.shape), _resident(dwb.shape), _resident(lng.shape),
                  _resident(lnb.shape), _resident(pw_bd.shape), _resident(ps.shape)],
        out_specs=pl.BlockSpec((None, ts, D_CONV + D_POOL), lambda bi, i: (bi, i, 0)),
        out_shape=jax.ShapeDtypeStruct((b, seq, D_CONV + D_POOL), BF16),
        scratch_shapes=[pltpu.VMEM((n, D_CONV), F32), pltpu.VMEM((SUBLANES - 1, n, D_CONV), F32),
                        pltpu.VMEM((n, D_POOL), F32), pltpu.VMEM((n, D_POOL), F32),
                        pltpu.VMEM((n, D_POOL), F32), pltpu.VMEM((n, D_POOL), F32)],
        compiler_params=pltpu.CompilerParams(dimension_semantics=("arbitrary", "arbitrary"),
                                             vmem_limit_bytes=VMEM_LIMIT_BYTES),
        name="convpool",
    )(ucp, ucp, ucp, dw, dwb, lng, lnb, pw_bd, ps)


def _attn_kernel(lambda_init, unshifted, slopes_ref, qT_ref, ka_ref, vT_ref, lq1_ref, lk1_ref,
                 lq2_ref, lk2_ref, sub_ref, out_ref, qv_ref, m_ref, acc_ref):
    h = pl.program_id(1)
    i = pl.program_id(2)
    nk = ka_ref.shape[1]
    tk = ka_ref.shape[2]
    tq = qT_ref.shape[2]
    slope = slopes_ref[h]

    row = lax.broadcasted_iota(jnp.int32, (V_DIM, tq), 0)
    for c in range(2):
        q = qT_ref[c]
        qv_ref[0, c] = jnp.where(row < HEAD_DIM, q, -q)
        qv_ref[1, c] = jnp.where(row < HEAD_DIM, q, jnp.zeros_like(q))
        qv_ref[2, c] = q
    acc_ref[...] = jnp.zeros(acc_ref.shape, F32)

    def diag_unit():
        kk = lax.broadcasted_iota(jnp.int32, (tk, tq), 0)
        qq = lax.broadcasted_iota(jnp.int32, (tk, tq), 1)
        return i, 1, slope * jnp.abs(kk - qq).astype(F32)

    def off_diag_unit(t):
        j = jnp.where(t >= i, t + 1, t)
        dist = jnp.abs(j - i) * tk
        return j, jnp.where(j > i, 2, 0), slope * jnp.full((1, tq), dist, jnp.int32).astype(F32)

    def scores(unit, c):
        j, side, bias = unit
        return jnp.dot(ka_ref[c, j], qv_ref[side, c], preferred_element_type=F32) - bias

    def unshifted_chunk(units):
        chains = [(u, c) for u in units for c in range(2)]
        probs = [None] * len(chains)
        pv = [[], []]

        def score_stage(k):
            probs[k] = jnp.exp(scores(*chains[k])).astype(BF16)

        def pv_stage(k):
            (j, _, _), c = chains[k]
            pv[c].append(jnp.dot(vT_ref[j], probs[k], preferred_element_type=F32))

        score_stage(0)
        for k in range(1, len(chains)):
            score_stage(k)
            pv_stage(k - 1)
        pv_stage(len(chains) - 1)
        for c in range(2):
            acc_ref[c] += functools.reduce(lambda a, b: a + b, pv[c])

    def running_max_step(unit):
        for c in range(2):
            s = scores(unit, c)
            m_old = m_ref[c]
            m_new = jnp.maximum(m_old, jnp.max(s, axis=0, keepdims=True))
            p = jnp.exp(s - m_new).astype(BF16)
            alpha = jnp.exp(m_old - m_new)
            acc_ref[c] = acc_ref[c] * alpha + jnp.dot(vT_ref[unit[0]], p, preferred_element_type=F32)
            m_ref[c] = m_new

    if unshifted:
        chunk = min(UNSHIFTED_CHUNK, nk)
        unshifted_chunk([diag_unit()] + [off_diag_unit(t) for t in range(chunk - 1)])

        def body(g, carry):
            t0 = chunk - 1 + g * chunk
            unshifted_chunk([off_diag_unit(t0 + r) for r in range(chunk)])
            return carry

        lax.fori_loop(0, nk // chunk - 1, body, 0)
    else:
        m_ref[...] = jnp.full(m_ref.shape, NEG_BIG, F32)
        running_max_step(diag_unit())

        def body(t, carry):
            running_max_step(off_diag_unit(t))
            return carry

        lax.fori_loop(0, nk - 1, body, 0)

    lam = (jnp.exp(jnp.sum(lq1_ref[...] * lk1_ref[...], axis=-1, keepdims=True))
           - jnp.exp(jnp.sum(lq2_ref[...] * lk2_ref[...], axis=-1, keepdims=True))
           + lambda_init)
    a0 = acc_ref[0]
    a1 = acc_ref[1]
    o = a0[:V_DIM] / a0[V_DIM:V_DIM + 1] - lam * (a1[:V_DIM] / a1[V_DIM:V_DIM + 1])
    ms = jnp.mean(o * o, axis=0, keepdims=True)
    y = o * lax.rsqrt(ms + NORM_EPS) * sub_ref[...] * (1.0 - lambda_init)
    out_ref[...] = y.T.astype(out_ref.dtype)


def _attention(qT, ka, vT, lq1, lk1, lq2, lk2, sub_col, lambda_init, unshifted):
    b, nh, _, kw, seq = qT.shape
    blk = vT.shape[-1]
    nblk = seq // blk
    assert kw == V_DIM and nblk % min(UNSHIFTED_CHUNK, nblk) == 0
    ka = ka.reshape(b, nh, 2, nblk, blk, kw)
    slopes = jnp.asarray([_slope(h) for h in range(nh)], F32)
    return pl.pallas_call(
        functools.partial(_attn_kernel, lambda_init, unshifted),
        grid=(b, nh, nblk),
        in_specs=[pl.BlockSpec(memory_space=pltpu.SMEM),
                  pl.BlockSpec((None, None, 2, kw, blk), lambda bi, h, i: (bi, h, 0, 0, i)),
                  pl.BlockSpec((None, None, 2, nblk, blk, kw), lambda bi, h, i: (bi, h, 0, 0, 0, 0)),
                  pl.BlockSpec((None, None, nblk, V_ROWS, blk), lambda bi, h, i: (bi, h, 0, 0, 0)),
                  _resident(lq1.shape), _resident(lk1.shape), _resident(lq2.shape),
                  _resident(lk2.shape), _resident(sub_col.shape)],
        out_specs=pl.BlockSpec((None, blk, V_DIM), lambda bi, h, i: (bi, i, h)),
        out_shape=jax.ShapeDtypeStruct((b, seq, nh * V_DIM), BF16),
        scratch_shapes=[pltpu.VMEM((3, 2, kw, blk), qT.dtype),
                        pltpu.VMEM((2, 1, blk), F32),
                        pltpu.VMEM((2, V_ROWS, blk), F32)],
        compiler_params=pltpu.CompilerParams(
            dimension_semantics=("arbitrary", "arbitrary", "arbitrary"),
            vmem_limit_bytes=VMEM_LIMIT_BYTES),
        name="attention_unshifted" if unshifted else "attention_running_max",
    )(slopes, qT, ka, vT, lq1, lk1, lq2, lk2, sub_col)


def _diff_attention(qT, ka, vT, q_gain, k_gain, lq1, lk1, lq2, lk2, sub_col, lambda_init):
    score_bound = (1.01 * HEAD_DIM ** 0.5) * jnp.max(jnp.abs(q_gain)) * jnp.max(jnp.abs(k_gain))

    def branch(unshifted):
        return lambda qT, ka, vT: _attention(qT, ka, vT, lq1, lk1, lq2, lk2, sub_col, lambda_init,
                                             unshifted)

    return lax.cond(score_bound <= MAX_UNSHIFTED_SCORE, branch(True), branch(False), qT, ka, vT)


def _out_ffn2_kernel(n_riders, x1_ref, ycp_ref, yat_ref, wo_cp_ref, wo_at_ref, g2_ref, wg_ref, wu_ref,
                     wd_ref, gp_ref, *refs):
    out_ref = refs[n_riders]
    _cast_riders(refs[:n_riders], refs[n_riders + 1:])
    halves = _row_halves(x1_ref.shape[0])
    x2s = [x1_ref[r, :]
           + jnp.dot(ycp_ref[r, :], wo_cp_ref[...], preferred_element_type=F32)
           + jnp.dot(yat_ref[r, :], wo_at_ref[...], preferred_element_type=F32) for r in halves]
    ys = _swiglu_halves([_rms(x2, g2_ref[...]) for x2 in x2s], wg_ref, wu_ref, wd_ref)
    for r, x2, y in zip(halves, x2s, ys):
        out_ref[r, :] = _rms(x2 + 0.5 * y, gp_ref[...])


def _out_ffn2(x1, ycp, yat, wo_cp, wo_at, g2, wg, wu, wd, gp, riders):
    rows, d = x1.shape
    f = wg.shape[1]
    tm = _row_block(rows)
    row_spec = lambda w: pl.BlockSpec((tm, w), lambda i: (i, 0))
    views, r_in, r_out, r_shapes = _rider_specs(riders, d, rows // tm)
    outs = pl.pallas_call(
        functools.partial(_out_ffn2_kernel, len(riders)),
        grid=(rows // tm,),
        in_specs=[row_spec(d), row_spec(ycp.shape[1]), row_spec(yat.shape[1]),
                  _resident(wo_cp.shape), _resident(wo_at.shape), _resident((1, d)),
                  _resident((d, f)), _resident((d, f)), _resident((f, d)), _resident((1, d))] + r_in,
        out_specs=[row_spec(d)] + r_out,
        out_shape=[jax.ShapeDtypeStruct((rows, d), F32)] + r_shapes,
        compiler_params=pltpu.CompilerParams(dimension_semantics=("arbitrary",),
                                             vmem_limit_bytes=VMEM_LIMIT_BYTES),
        name="out_ffn2",
    )(x1, ycp, yat, wo_cp, wo_at, g2, wg, wu, wd, gp, *views)
    return outs[0], [o.reshape(w.shape) for o, w in zip(outs[1:], riders)]


def _pool_block_diag(pool_w):
    g, cg, _ = pool_w.shape
    out = jnp.zeros((g * cg, g * cg), pool_w.dtype)
    for k in range(g):
        out = out.at[k * cg:(k + 1) * cg, k * cg:(k + 1) * cg].set(pool_w[k])
    return out


def kernel(x, ffn1_norm, ffn1_w_gate, ffn1_w_up, ffn1_w_down, mix_norm, w_in, conv_dw, conv_dw_bias, conv_ln_gain, conv_ln_bias, pool_w, pool_scale, q_norm, k_norm, lambda_q1, lambda_k1, lambda_q2, lambda_k2, attn_subln, w_out, ffn2_norm, ffn2_w_gate, ffn2_w_up, ffn2_w_down, post_norm):
    b, seq, d = x.shape
    depth = w_in.shape[0]
    rows = b * seq
    d_cp = D_CONV + D_POOL
    row = lambda v: v.reshape(1, -1).astype(F32)
    xf = x.reshape(rows, d).astype(F32)
    stage1 = [w[0].astype(BF16) for w in (ffn1_w_gate, ffn1_w_up, ffn1_w_down, w_in)]
    for l in range(depth):
        lambda_init = 0.8 - 0.6 * math.exp(-0.3 * l)
        pair = lambda g: jnp.concatenate([g, g]).reshape(1, -1).astype(F32)
        x1, ucp, qT, ka, vT, stage3 = _ffn1_win(
            xf, row(ffn1_norm[l]), stage1[0], stage1[1], stage1[2], row(mix_norm[l]), stage1[3],
            pair(q_norm[l]), pair(k_norm[l]), seq,
            [ffn2_w_gate[l], ffn2_w_up[l], ffn2_w_down[l], w_out[l]])
        ycp = _convpool(
            ucp.reshape(b, seq, -1), conv_dw[l].astype(F32), row(conv_dw_bias[l]),
            row(conv_ln_gain[l]), row(conv_ln_bias[l]),
            _pool_block_diag(pool_w[l]).astype(BF16), row(pool_scale[l]))
        yat = _diff_attention(
            qT, ka, vT, q_norm[l], k_norm[l], row(lambda_q1[l]), row(lambda_k1[l]), row(lambda_q2[l]),
            row(lambda_k2[l]), attn_subln[l].reshape(-1, 1).astype(F32), lambda_init)
        wo = stage3[3]
        nxt = l + 1
        xf, stage1 = _out_ffn2(
            x1, ycp.reshape(rows, d_cp), yat.reshape(rows, -1), wo[:d_cp], wo[d_cp:],
            row(ffn2_norm[l]), stage3[0], stage3[1], stage3[2], row(post_norm[l]),
            [w[nxt] for w in (ffn1_w_gate, ffn1_w_up, ffn1_w_down, w_in)] if nxt < depth else [])
    return xf.reshape(b, seq, d).astype(x.dtype)
```

```python
import functools
import math

import jax
import jax.numpy as jnp
from jax import lax
from jax.experimental import pallas as pl
from jax.experimental.pallas import tpu as pltpu

F32 = jnp.float32
BF16 = jnp.bfloat16

NORM_EPS = 1e-6
N_HEADS = 4
HEAD_DIM = 64
V_DIM = 2 * HEAD_DIM
D_CONV = 256
D_POOL = 256
D_ATTN = 512
CONV_WIDTH = 31
POOL_GROUP_DIM = 64
SUBLANES = 8
HALO = 16
V_ROWS = V_DIM + 8
BF16_EXACT_INT = 256
VMEM_LIMIT_BYTES = 56 * 1024 * 1024
NEG_BIG = -1e30
MAX_UNSHIFTED_SCORE = 60.0
UNSHIFTED_CHUNK = 16


def _attn_block(seq):
    return min(512, seq)


def _row_block(rows):
    return min(512, rows)


def _rms(x, g):
    ms = jnp.mean(x * x, axis=-1, keepdims=True)
    return x * lax.rsqrt(ms + NORM_EPS) * g


def _sigmoid(x):
    return 1.0 / (1.0 + jnp.exp(-x))


def _row_halves(tm):
    half = tm // 2 if tm % (2 * SUBLANES) == 0 else tm
    return [slice(r, r + half) for r in range(0, tm, half)]


def _swiglu_halves(hs, wg_ref, wu_ref, wd_ref):
    hbs = [h.astype(BF16) for h in hs]
    gus = [(jnp.dot(hb, wg_ref[...], preferred_element_type=F32),
            jnp.dot(hb, wu_ref[...], preferred_element_type=F32)) for hb in hbs]
    acts = [(g * _sigmoid(g) * u).astype(BF16) for g, u in gus]
    return [jnp.dot(a, wd_ref[...], preferred_element_type=F32) for a in acts]


def _resident(shape):
    nd = len(shape)
    return pl.BlockSpec(shape, lambda *_: (0,) * nd, pipeline_mode=pl.Buffered(1))


def _slope(h):
    return 2.0 ** (-8.0 * (h + 1) / N_HEADS)


def _attention_operands(uq, uk, uv, rows, gq_ref, gk_ref, qT_ref, ka_ref, vT_ref):
    n = uq.shape[0]
    blk = qT_ref.shape[-1]
    lane = lax.broadcasted_iota(jnp.int32, (n, V_DIM), 1)
    low = lane < HEAD_DIM
    pos = lax.broadcasted_iota(jnp.int32, (n, V_DIM), 0) + rows.start
    one = jnp.ones((n, V_DIM), F32)
    zero = jnp.zeros((n, V_DIM), F32)
    row8 = lax.broadcasted_iota(jnp.int32, (V_ROWS - V_DIM, n), 0)
    ones_rows = jnp.where(row8 == 0, 1.0, 0.0).astype(vT_ref.dtype)
    digits = [(pos % BF16_EXACT_INT).astype(F32)]
    place = BF16_EXACT_INT
    while place < blk:
        digits.append((pos % (place * BF16_EXACT_INT) - pos % place).astype(F32))
        place *= BF16_EXACT_INT

    def norm_pair(x, g):
        x2 = x * x
        s_lo = jnp.sum(jnp.where(low, x2, 0.0), axis=-1, keepdims=True)
        s_hi = jnp.sum(jnp.where(low, 0.0, x2), axis=-1, keepdims=True)
        ms = jnp.where(low, s_lo, s_hi) * (1.0 / HEAD_DIM)
        return x * lax.rsqrt(ms + NORM_EPS) * g

    def aug(cols):
        a = zero
        for k, col in enumerate(cols):
            a = jnp.where(lane == HEAD_DIM + k, col, a)
        return a

    def other_map(x):
        return pltpu.roll(x, HEAD_DIM, 1)

    for h in range(N_HEADS):
        cols = slice(h * V_DIM, (h + 1) * V_DIM)
        qn = norm_pair(uq[:, cols], gq_ref[...]) * (HEAD_DIM ** -0.5)
        kn = norm_pair(uk[:, cols], gk_ref[...])
        coef = _slope(h)
        q_aug = aug([-coef * one] * len(digits) + digits)
        k_aug = aug(digits + [coef * one] * len(digits))
        for c in range(2):
            qc = qn if c == 0 else other_map(qn)
            kc = kn if c == 0 else other_map(kn)
            qT_ref[h, c, :, rows] = jnp.where(low, qc, q_aug).T.astype(qT_ref.dtype)
            ka_ref[h, c, rows, :] = jnp.where(low, kc, k_aug).astype(ka_ref.dtype)
        vT_ref[h, :V_DIM, rows] = uv[:, cols].T.astype(vT_ref.dtype)
        vT_ref[h, V_DIM:, rows] = ones_rows


def _rider_specs(stacked, layer, n, steps):
    views, in_specs, out_specs, out_shapes = [], [], [], []
    rpb = n // steps
    assert n % steps == 0 and rpb % (2 * SUBLANES) == 0
    for w in stacked:
        v = w.reshape(w.shape[0], n, -1)
        views.append(v)
        in_specs.append(pl.BlockSpec((None, rpb, v.shape[2]), lambda i: (layer, i, 0)))
        out_specs.append(pl.BlockSpec((rpb, v.shape[2]), lambda i: (i, 0)))
        out_shapes.append(jax.ShapeDtypeStruct(v.shape[1:], BF16))
    return views, in_specs, out_specs, out_shapes


def _cast_riders(in_refs, out_refs):
    for src, dst in zip(in_refs, out_refs):
        dst[...] = src[...].astype(dst.dtype)


def _ffn1_win_kernel(n_riders, x_ref, g1_ref, wg_ref, wu_ref, wd_ref, gm_ref, win_ref, gq_ref, gk_ref,
                     *refs):
    x1_ref, ucp_ref, qT_ref, ka_ref, vT_ref = refs[n_riders:n_riders + 5]
    _cast_riders(refs[:n_riders], refs[n_riders + 5:])
    halves = _row_halves(x_ref.shape[0])
    xs = [x_ref[r, :] for r in halves]
    ys = _swiglu_halves([_rms(x, g1_ref[...]) for x in xs], wg_ref, wu_ref, wd_ref)
    x1s = [x + 0.5 * y for x, y in zip(xs, ys)]
    hms = [_rms(x1, gm_ref[...]).astype(BF16) for x1 in x1s]
    c0 = 2 * D_CONV + D_POOL
    u_attn = [jnp.dot(hm, win_ref[:, c0:], preferred_element_type=F32) for hm in hms]
    for r, u in zip(halves, u_attn):
        _attention_operands(u[:, :D_ATTN], u[:, D_ATTN:2 * D_ATTN], u[:, 2 * D_ATTN:], r,
                            gq_ref, gk_ref, qT_ref, ka_ref, vT_ref)
    for r, x1, hm in zip(halves, x1s, hms):
        x1_ref[r, :] = x1
        ucp_ref[r, :] = jnp.dot(hm, win_ref[:, :c0], preferred_element_type=F32)


def _ffn1_win(x, g1, wg, wu, wd, gm, win, gq, gk, seq, riders, layer):
    rows, d = x.shape
    f = wg.shape[1]
    tm = _row_block(rows)
    c0 = 2 * D_CONV + D_POOL
    nblk = seq // tm
    assert tm == _attn_block(seq) and rows % seq == 0
    row_spec = lambda w: pl.BlockSpec((tm, w), lambda i: (i, 0))
    views, r_in, r_out, r_shapes = _rider_specs(riders, layer, d, rows // tm)
    outs = pl.pallas_call(
        functools.partial(_ffn1_win_kernel, len(riders)),
        grid=(rows // tm,),
        in_specs=[row_spec(d), _resident((1, d)), _resident((d, f)), _resident((d, f)),
                  _resident((f, d)), _resident((1, d)), _resident(win.shape),
                  _resident(gq.shape), _resident(gk.shape)] + r_in,
        out_specs=[row_spec(d), row_spec(c0),
                   pl.BlockSpec((None, N_HEADS, 2, V_DIM, tm), lambda i: (i // nblk, 0, 0, 0, i % nblk)),
                   pl.BlockSpec((None, N_HEADS, 2, tm, V_DIM), lambda i: (i // nblk, 0, 0, i % nblk, 0)),
                   pl.BlockSpec((None, N_HEADS, None, V_ROWS, tm),
                                lambda i: (i // nblk, 0, i % nblk, 0, 0))] + r_out,
        out_shape=[jax.ShapeDtypeStruct((rows, d), F32),
                   jax.ShapeDtypeStruct((rows, c0), F32),
                   jax.ShapeDtypeStruct((rows // seq, N_HEADS, 2, V_DIM, seq), BF16),
                   jax.ShapeDtypeStruct((rows // seq, N_HEADS, 2, seq, V_DIM), BF16),
                   jax.ShapeDtypeStruct((rows // seq, N_HEADS, nblk, V_ROWS, tm), BF16)] + r_shapes,
        compiler_params=pltpu.CompilerParams(dimension_semantics=("arbitrary",),
                                             vmem_limit_bytes=VMEM_LIMIT_BYTES),
        name="ffn1_win",
    )(x, g1, wg, wu, wd, gm, win, gq, gk, *views)
    return (*outs[:5], [o.reshape(w.shape[1:]) for o, w in zip(outs[5:], riders)])


def _convpool_tile(t0, seq, main_ref, prev_ref, next_ref, params, scratch, out_ref):
    dw_ref, dwb_ref, lng_ref, lnb_ref, pw_ref, ps_ref = params
    zext, zrot, pext, a1, a2, a3 = scratch
    ts = main_ref.shape[0]
    has_prev = t0 > 0
    has_next = t0 + ts < seq

    def glu(u):
        return u[:, :D_CONV] * _sigmoid(u[:, D_CONV:2 * D_CONV])

    def assemble():
        um = main_ref[...]
        up = prev_ref[...]
        un = next_ref[...]
        zero_halo = jnp.zeros((HALO, D_CONV), F32)
        zext[0:HALO, :] = jnp.where(has_prev, glu(up), zero_halo)
        zext[HALO:HALO + ts, :] = glu(um)
        zext[HALO + ts:, :] = jnp.where(has_next, glu(un), zero_halo)
        pext[0:HALO, :] = jnp.where(has_prev, up[:, 2 * D_CONV:], zero_halo)
        pext[HALO:HALO + ts, :] = um[:, 2 * D_CONV:]
        pext[HALO + ts:, :] = jnp.where(has_next, un[:, 2 * D_CONV:], zero_halo)

    def conv():
        off0 = HALO - CONV_WIDTH // 2
        nz = ts + 2 * HALO - SUBLANES
        for r in range(1, SUBLANES):
            zrot[r - 1, 0:nz, :] = zext[r:r + nz, :]
        acc = jnp.zeros((ts, D_CONV), F32)
        for j in range(CONV_WIDTH):
            r, a = (off0 + j) % SUBLANES, (off0 + j) // SUBLANES * SUBLANES
            tap = zext[a:a + ts, :] if r == 0 else zrot[r - 1, a:a + ts, :]
            acc = acc + dw_ref[j:j + 1, :] * tap
        z = acc + dwb_ref[...]
        mu = jnp.mean(z, axis=-1, keepdims=True)
        zc = z - mu
        var = jnp.mean(zc * zc, axis=-1, keepdims=True)
        y = zc * lax.rsqrt(var + NORM_EPS) * lng_ref[...] + lnb_ref[...]
        out_ref[:, :D_CONV] = (y * _sigmoid(y)).astype(out_ref.dtype)

    def pool():
        n = ts + 2 * HALO
        a1[1:n, :] = pext[0:n - 1, :] + pext[1:n, :]
        a2[2:n - 1, :] = a1[1:n - 2, :] + a1[3:n, :]
        a3[4:n - 3, :] = a2[2:n - 5, :] + a2[6:n - 1, :]
        w2 = a1[HALO:HALO + ts, :]
        w4 = a2[HALO:HALO + ts, :]
        w8 = a3[HALO:HALO + ts, :]
        w16 = a3[HALO - 4:HALO - 4 + ts, :] + a3[HALO + 4:HALO + 4 + ts, :]
        grp = lax.broadcasted_iota(jnp.int32, (ts, D_POOL), 1) // POOL_GROUP_DIM
        win = jnp.where(grp == 0, w2, jnp.where(grp == 1, w4, jnp.where(grp == 2, w8, w16)))
        t = lax.broadcasted_iota(jnp.int32, (ts, D_POOL), 0) + t0
        half = jnp.left_shift(1, grp)
        cnt = jnp.minimum(t + half, seq) - jnp.maximum(t - half, 0)
        d = win / cnt.astype(F32) - pext[HALO:HALO + ts, :]
        yp = jnp.dot(d.astype(BF16), pw_ref[...], preferred_element_type=F32) * ps_ref[...]
        out_ref[:, D_CONV:] = yp.astype(out_ref.dtype)

    return [assemble, conv, pool]


def _convpool_kernel(seq, main_ref, prev_ref, next_ref, dw_ref, dwb_ref, lng_ref, lnb_ref,
                     pw_ref, ps_ref, out_ref, *scratch):
    t0 = pl.program_id(1) * main_ref.shape[0]
    for stage in _convpool_tile(t0, seq, main_ref, prev_ref, next_ref,
                                (dw_ref, dwb_ref, lng_ref, lnb_ref, pw_ref, ps_ref), scratch, out_ref):
        stage()


def _convpool(ucp, dw, dwb, lng, lnb, pw_bd, ps):
    b, seq, c0 = ucp.shape
    ts = _attn_block(seq)
    nh = ts // HALO
    last_halo = seq // HALO - 1
    n = ts + 2 * HALO
    return pl.pallas_call(
        functools.partial(_convpool_kernel, seq),
        grid=(b, seq // ts),
        in_specs=[pl.BlockSpec((None, ts, c0), lambda bi, i: (bi, i, 0)),
                  pl.BlockSpec((None, HALO, c0), lambda bi, i: (bi, jnp.maximum(i * nh - 1, 0), 0)),
                  pl.BlockSpec((None, HALO, c0),
                               lambda bi, i: (bi, jnp.minimum((i + 1) * nh, last_halo), 0)),
                  _resident(dw.shape), _resident(dwb.shape), _resident(lng.shape),
                  _resident(lnb.shape), _resident(pw_bd.shape), _resident(ps.shape)],
        out_specs=pl.BlockSpec((None, ts, D_CONV + D_POOL), lambda bi, i: (bi, i, 0)),
        out_shape=jax.ShapeDtypeStruct((b, seq, D_CONV + D_POOL), BF16),
        scratch_shapes=[pltpu.VMEM((n, D_CONV), F32), pltpu.VMEM((SUBLANES - 1, n, D_CONV), F32),
                        pltpu.VMEM((n, D_POOL), F32), pltpu.VMEM((n, D_POOL), F32),
                        pltpu.VMEM((n, D_POOL), F32), pltpu.VMEM((n, D_POOL), F32)],
        compiler_params=pltpu.CompilerParams(dimension_semantics=("arbitrary", "arbitrary"),
                                             vmem_limit_bytes=VMEM_LIMIT_BYTES),
        name="convpool",
    )(ucp, ucp, ucp, dw, dwb, lng, lnb, pw_bd, ps)


def _attn_kernel(lambda_init, unshifted, slopes_ref, qT_ref, ka_ref, vT_ref, lq1_ref, lk1_ref,
                 lq2_ref, lk2_ref, sub_ref, out_ref, qv_ref, m_ref, acc_ref):
    h = pl.program_id(1)
    i = pl.program_id(2)
    nk = ka_ref.shape[1]
    tk = ka_ref.shape[2]
    tq = qT_ref.shape[2]
    slope = slopes_ref[h]

    row = lax.broadcasted_iota(jnp.int32, (V_DIM, tq), 0)
    for c in range(2):
        q = qT_ref[c]
        qv_ref[0, c] = jnp.where(row < HEAD_DIM, q, -q)
        qv_ref[1, c] = jnp.where(row < HEAD_DIM, q, jnp.zeros_like(q))
        qv_ref[2, c] = q
    acc_ref[...] = jnp.zeros(acc_ref.shape, F32)

    def diag_unit():
        kk = lax.broadcasted_iota(jnp.int32, (tk, tq), 0)
        qq = lax.broadcasted_iota(jnp.int32, (tk, tq), 1)
        return i, 1, slope * jnp.abs(kk - qq).astype(F32)

    def off_diag_unit(t):
        j = jnp.where(t >= i, t + 1, t)
        dist = jnp.abs(j - i) * tk
        return j, jnp.where(j > i, 2, 0), slope * jnp.full((1, tq), dist, jnp.int32).astype(F32)

    def scores(unit, c):
        j, side, bias = unit
        return jnp.dot(ka_ref[c, j], qv_ref[side, c], preferred_element_type=F32) - bias

    def unshifted_chunk(units):
        chains = [(u, c) for u in units for c in range(2)]
        probs = [None] * len(chains)
        pv = [[], []]

        def score_stage(k):
            probs[k] = jnp.exp(scores(*chains[k])).astype(BF16)

        def pv_stage(k):
            (j, _, _), c = chains[k]
            pv[c].append(jnp.dot(vT_ref[j], probs[k], preferred_element_type=F32))

        score_stage(0)
        for k in range(1, len(chains)):
            score_stage(k)
            pv_stage(k - 1)
        pv_stage(len(chains) - 1)
        for c in range(2):
            acc_ref[c] += functools.reduce(lambda a, b: a + b, pv[c])

    def running_max_step(unit):
        for c in range(2):
            s = scores(unit, c)
            m_old = m_ref[c]
            m_new = jnp.maximum(m_old, jnp.max(s, axis=0, keepdims=True))
            p = jnp.exp(s - m_new).astype(BF16)
            alpha = jnp.exp(m_old - m_new)
            acc_ref[c] = acc_ref[c] * alpha + jnp.dot(vT_ref[unit[0]], p, preferred_element_type=F32)
            m_ref[c] = m_new

    if unshifted:
        chunk = min(UNSHIFTED_CHUNK, nk)
        unshifted_chunk([diag_unit()] + [off_diag_unit(t) for t in range(chunk - 1)])

        def body(g, carry):
            t0 = chunk - 1 + g * chunk
            unshifted_chunk([off_diag_unit(t0 + r) for r in range(chunk)])
            return carry

        lax.fori_loop(0, nk // chunk - 1, body, 0)
    else:
        m_ref[...] = jnp.full(m_ref.shape, NEG_BIG, F32)
        running_max_step(diag_unit())

        def body(t, carry):
            running_max_step(off_diag_unit(t))
            return carry

        lax.fori_loop(0, nk - 1, body, 0)

    lam = (jnp.exp(jnp.sum(lq1_ref[...] * lk1_ref[...], axis=-1, keepdims=True))
           - jnp.exp(jnp.sum(lq2_ref[...] * lk2_ref[...], axis=-1, keepdims=True))
           + lambda_init)
    a0 = acc_ref[0]
    a1 = acc_ref[1]
    o = a0[:V_DIM] / a0[V_DIM:V_DIM + 1] - lam * (a1[:V_DIM] / a1[V_DIM:V_DIM + 1])
    ms = jnp.mean(o * o, axis=0, keepdims=True)
    y = o * lax.rsqrt(ms + NORM_EPS) * sub_ref[...] * (1.0 - lambda_init)
    out_ref[...] = y.T.astype(out_ref.dtype)


def _attention(qT, ka, vT, lq1, lk1, lq2, lk2, sub_col, lambda_init, unshifted):
    b, nh, _, kw, seq = qT.shape
    blk = vT.shape[-1]
    nblk = seq // blk
    assert kw == V_DIM and nblk % min(UNSHIFTED_CHUNK, nblk) == 0
    ka = ka.reshape(b, nh, 2, nblk, blk, kw)
    slopes = jnp.asarray([_slope(h) for h in range(nh)], F32)
    return pl.pallas_call(
        functools.partial(_attn_kernel, lambda_init, unshifted),
        grid=(b, nh, nblk),
        in_specs=[pl.BlockSpec(memory_space=pltpu.SMEM),
                  pl.BlockSpec((None, None, 2, kw, blk), lambda bi, h, i: (bi, h, 0, 0, i)),
                  pl.BlockSpec((None, None, 2, nblk, blk, kw), lambda bi, h, i: (bi, h, 0, 0, 0, 0)),
                  pl.BlockSpec((None, None, nblk, V_ROWS, blk), lambda bi, h, i: (bi, h, 0, 0, 0)),
                  _resident(lq1.shape), _resident(lk1.shape), _resident(lq2.shape),
                  _resident(lk2.shape), _resident(sub_col.shape)],
        out_specs=pl.BlockSpec((None, blk, V_DIM), lambda bi, h, i: (bi, i, h)),
        out_shape=jax.ShapeDtypeStruct((b, seq, nh * V_DIM), BF16),
        scratch_shapes=[pltpu.VMEM((3, 2, kw, blk), qT.dtype),
                        pltpu.VMEM((2, 1, blk), F32),
                        pltpu.VMEM((2, V_ROWS, blk), F32)],
        compiler_params=pltpu.CompilerParams(
            dimension_semantics=("arbitrary", "arbitrary", "arbitrary"),
            vmem_limit_bytes=VMEM_LIMIT_BYTES),
        name="attention_unshifted" if unshifted else "attention_running_max",
    )(slopes, qT, ka, vT, lq1, lk1, lq2, lk2, sub_col)


def _diff_attention(qT, ka, vT, q_gain, k_gain, lq1, lk1, lq2, lk2, sub_col, lambda_init):
    score_bound = (1.01 * HEAD_DIM ** 0.5) * jnp.max(jnp.abs(q_gain)) * jnp.max(jnp.abs(k_gain))

    def branch(unshifted):
        return lambda qT, ka, vT: _attention(qT, ka, vT, lq1, lk1, lq2, lk2, sub_col, lambda_init,
                                             unshifted)

    return lax.cond(score_bound <= MAX_UNSHIFTED_SCORE, branch(True), branch(False), qT, ka, vT)


def _out_ffn2_kernel(n_riders, x1_ref, ycp_ref, yat_ref, wo_cp_ref, wo_at_ref, g2_ref, wg_ref, wu_ref,
                     wd_ref, gp_ref, *refs):
    out_ref = refs[n_riders]
    _cast_riders(refs[:n_riders], refs[n_riders + 1:])
    halves = _row_halves(x1_ref.shape[0])
    x2s = [x1_ref[r, :]
           + jnp.dot(ycp_ref[r, :], wo_cp_ref[...], preferred_element_type=F32)
           + jnp.dot(yat_ref[r, :], wo_at_ref[...], preferred_element_type=F32) for r in halves]
    ys = _swiglu_halves([_rms(x2, g2_ref[...]) for x2 in x2s], wg_ref, wu_ref, wd_ref)
    for r, x2, y in zip(halves, x2s, ys):
        out_ref[r, :] = _rms(x2 + 0.5 * y, gp_ref[...])


def _out_ffn2(x1, ycp, yat, wo_cp, wo_at, g2, wg, wu, wd, gp, riders, layer):
    rows, d = x1.shape
    f = wg.shape[1]
    tm = _row_block(rows)
    row_spec = lambda w: pl.BlockSpec((tm, w), lambda i: (i, 0))
    views, r_in, r_out, r_shapes = _rider_specs(riders, layer, d, rows // tm)
    outs = pl.pallas_call(
        functools.partial(_out_ffn2_kernel, len(riders)),
        grid=(rows // tm,),
        in_specs=[row_spec(d), row_spec(ycp.shape[1]), row_spec(yat.shape[1]),
                  _resident(wo_cp.shape), _resident(wo_at.shape), _resident((1, d)),
                  _resident((d, f)), _resident((d, f)), _resident((f, d)), _resident((1, d))] + r_in,
        out_specs=[row_spec(d)] + r_out,
        out_shape=[jax.ShapeDtypeStruct((rows, d), F32)] + r_shapes,
        compiler_params=pltpu.CompilerParams(dimension_semantics=("arbitrary",),
                                             vmem_limit_bytes=VMEM_LIMIT_BYTES),
        name="out_ffn2",
    )(x1, ycp, yat, wo_cp, wo_at, g2, wg, wu, wd, gp, *views)
    return outs[0], [o.reshape(w.shape[1:]) for o, w in zip(outs[1:], riders)]


def _pool_block_diag(pool_w):
    g, cg, _ = pool_w.shape
    out = jnp.zeros((g * cg, g * cg), pool_w.dtype)
    for k in range(g):
        out = out.at[k * cg:(k + 1) * cg, k * cg:(k + 1) * cg].set(pool_w[k])
    return out


def kernel(x, ffn1_norm, ffn1_w_gate, ffn1_w_up, ffn1_w_down, mix_norm, w_in, conv_dw, conv_dw_bias, conv_ln_gain, conv_ln_bias, pool_w, pool_scale, q_norm, k_norm, lambda_q1, lambda_k1, lambda_q2, lambda_k2, attn_subln, w_out, ffn2_norm, ffn2_w_gate, ffn2_w_up, ffn2_w_down, post_norm):
    b, seq, d = x.shape
    depth = w_in.shape[0]
    rows = b * seq
    d_cp = D_CONV + D_POOL
    row = lambda v: v.reshape(1, -1).astype(F32)
    xf = x.reshape(rows, d).astype(F32)
    stage1 = [w[0].astype(BF16) for w in (ffn1_w_gate, ffn1_w_up, ffn1_w_down, w_in)]
    for l in range(depth):
        lambda_init = 0.8 - 0.6 * math.exp(-0.3 * l)
        pair = lambda g: jnp.concatenate([g, g]).reshape(1, -1).astype(F32)
        x1, ucp, qT, ka, vT, stage3 = _ffn1_win(
            xf, row(ffn1_norm[l]), stage1[0], stage1[1], stage1[2], row(mix_norm[l]), stage1[3],
            pair(q_norm[l]), pair(k_norm[l]), seq,
            [ffn2_w_gate, ffn2_w_up, ffn2_w_down, w_out], l)
        ycp = _convpool(
            ucp.reshape(b, seq, -1), conv_dw[l].astype(F32), row(conv_dw_bias[l]),
            row(conv_ln_gain[l]), row(conv_ln_bias[l]),
            _pool_block_diag(pool_w[l]).astype(BF16), row(pool_scale[l]))
        yat = _diff_attention(
            qT, ka, vT, q_norm[l], k_norm[l], row(lambda_q1[l]), row(lambda_k1[l]), row(lambda_q2[l]),
            row(lambda_k2[l]), attn_subln[l].reshape(-1, 1).astype(F32), lambda_init)
        wo = stage3[3]
        nxt = l + 1
        xf, stage1 = _out_ffn2(
            x1, ycp.reshape(rows, d_cp), yat.reshape(rows, -1), wo[:d_cp], wo[d_cp:],
            row(ffn2_norm[l]), stage3[0], stage3[1], stage3[2], row(post_norm[l]),
            [ffn1_w_gate, ffn1_w_up, ffn1_w_down, w_in] if nxt < depth else [], nxt)
    return xf.reshape(b, seq, d).astype(x.dtype)
```

```python
import functools
import math

import jax
import jax.numpy as jnp
from jax import lax
from jax.experimental import pallas as pl
from jax.experimental.pallas import tpu as pltpu

F32 = jnp.float32
BF16 = jnp.bfloat16

NORM_EPS = 1e-6
N_HEADS = 4
HEAD_DIM = 64
V_DIM = 2 * HEAD_DIM
D_CONV = 256
D_POOL = 256
D_ATTN = 512
CONV_WIDTH = 31
POOL_GROUP_DIM = 64
SUBLANES = 8
HALO = 16
V_ROWS = V_DIM + 8
BF16_EXACT_INT = 256
VMEM_LIMIT_BYTES = 56 * 1024 * 1024
NEG_BIG = -1e30
MAX_UNSHIFTED_SCORE = 60.0
UNSHIFTED_CHUNK = 16


def _attn_block(seq):
    return min(512, seq)


def _row_block(rows):
    return min(512, rows)


def _rms(x, g):
    ms = jnp.mean(x * x, axis=-1, keepdims=True)
    return x * lax.rsqrt(ms + NORM_EPS) * g


def _sigmoid(x):
    return 1.0 / (1.0 + jnp.exp(-x))


def _row_halves(tm):
    half = tm // 2 if tm % (2 * SUBLANES) == 0 else tm
    return [slice(r, r + half) for r in range(0, tm, half)]


def _swiglu_halves(hs, wg_ref, wu_ref, wd_ref):
    hbs = [h.astype(BF16) for h in hs]
    gus = [(jnp.dot(hb, wg_ref[...], preferred_element_type=F32),
            jnp.dot(hb, wu_ref[...], preferred_element_type=F32)) for hb in hbs]
    acts = [(g * _sigmoid(g) * u).astype(BF16) for g, u in gus]
    return [jnp.dot(a, wd_ref[...], preferred_element_type=F32) for a in acts]


def _resident(shape):
    nd = len(shape)
    return pl.BlockSpec(shape, lambda *_: (0,) * nd, pipeline_mode=pl.Buffered(1))


def _slope(h):
    return 2.0 ** (-8.0 * (h + 1) / N_HEADS)


def _attention_operands(uq, uk, uv, rows, gq_ref, gk_ref, qT_ref, ka_ref, vT_ref):
    n = uq.shape[0]
    blk = qT_ref.shape[-1]
    lane = lax.broadcasted_iota(jnp.int32, (n, V_DIM), 1)
    low = lane < HEAD_DIM
    pos = lax.broadcasted_iota(jnp.int32, (n, V_DIM), 0) + rows.start
    one = jnp.ones((n, V_DIM), F32)
    zero = jnp.zeros((n, V_DIM), F32)
    row8 = lax.broadcasted_iota(jnp.int32, (V_ROWS - V_DIM, n), 0)
    ones_rows = jnp.where(row8 == 0, 1.0, 0.0).astype(vT_ref.dtype)
    digits = [(pos % BF16_EXACT_INT).astype(F32)]
    place = BF16_EXACT_INT
    while place < blk:
        digits.append((pos % (place * BF16_EXACT_INT) - pos % place).astype(F32))
        place *= BF16_EXACT_INT

    def norm_pair(x, g):
        x2 = x * x
        s_lo = jnp.sum(jnp.where(low, x2, 0.0), axis=-1, keepdims=True)
        s_hi = jnp.sum(jnp.where(low, 0.0, x2), axis=-1, keepdims=True)
        ms = jnp.where(low, s_lo, s_hi) * (1.0 / HEAD_DIM)
        return x * lax.rsqrt(ms + NORM_EPS) * g

    def aug(cols):
        a = zero
        for k, col in enumerate(cols):
            a = jnp.where(lane == HEAD_DIM + k, col, a)
        return a

    def other_map(x):
        return pltpu.roll(x, HEAD_DIM, 1)

    for h in range(N_HEADS):
        cols = slice(h * V_DIM, (h + 1) * V_DIM)
        qn = norm_pair(uq[:, cols], gq_ref[...]) * (HEAD_DIM ** -0.5)
        kn = norm_pair(uk[:, cols], gk_ref[...])
        coef = _slope(h)
        q_aug = aug([-coef * one] * len(digits) + digits)
        k_aug = aug(digits + [coef * one] * len(digits))
        for c in range(2):
            qc = qn if c == 0 else other_map(qn)
            kc = kn if c == 0 else other_map(kn)
            qT_ref[h, c, :, rows] = jnp.where(low, qc, q_aug).T.astype(qT_ref.dtype)
            ka_ref[h, c, rows, :] = jnp.where(low, kc, k_aug).astype(ka_ref.dtype)
        vT_ref[h, :V_DIM, rows] = uv[:, cols].T.astype(vT_ref.dtype)
        vT_ref[h, V_DIM:, rows] = ones_rows


def _rider_specs(stacked, layer, steps):
    in_specs, out_specs, out_shapes = [], [], []
    for w in stacked:
        _, r, c = w.shape
        rpb = next(k for k in range(2 * SUBLANES, r + 1, 2 * SUBLANES)
                   if r % k == 0 and r // k <= steps)
        last = r // rpb - 1
        in_specs.append(pl.BlockSpec((None, rpb, c), lambda i, last=last: (layer, jnp.minimum(i, last), 0)))
        out_specs.append(pl.BlockSpec((rpb, c), lambda i, last=last: (jnp.minimum(i, last), 0)))
        out_shapes.append(jax.ShapeDtypeStruct((r, c), BF16))
    return in_specs, out_specs, out_shapes


def _cast_riders(in_refs, out_refs):
    for src, dst in zip(in_refs, out_refs):
        dst[...] = src[...].astype(dst.dtype)


def _ffn1_win_kernel(n_riders, x_ref, g1_ref, wg_ref, wu_ref, wd_ref, gm_ref, win_ref, gq_ref, gk_ref,
                     *refs):
    x1_ref, ucp_ref, qT_ref, ka_ref, vT_ref = refs[n_riders:n_riders + 5]
    _cast_riders(refs[:n_riders], refs[n_riders + 5:])
    halves = _row_halves(x_ref.shape[0])
    xs = [x_ref[r, :] for r in halves]
    ys = _swiglu_halves([_rms(x, g1_ref[...]) for x in xs], wg_ref, wu_ref, wd_ref)
    x1s = [x + 0.5 * y for x, y in zip(xs, ys)]
    hms = [_rms(x1, gm_ref[...]).astype(BF16) for x1 in x1s]
    c0 = 2 * D_CONV + D_POOL
    u_attn = [jnp.dot(hm, win_ref[:, c0:], preferred_element_type=F32) for hm in hms]
    for r, u in zip(halves, u_attn):
        _attention_operands(u[:, :D_ATTN], u[:, D_ATTN:2 * D_ATTN], u[:, 2 * D_ATTN:], r,
                            gq_ref, gk_ref, qT_ref, ka_ref, vT_ref)
    for r, x1, hm in zip(halves, x1s, hms):
        x1_ref[r, :] = x1
        ucp_ref[r, :] = jnp.dot(hm, win_ref[:, :c0], preferred_element_type=F32)


def _ffn1_win(x, g1, wg, wu, wd, gm, win, gq, gk, seq, riders, layer):
    rows, d = x.shape
    f = wg.shape[1]
    tm = _row_block(rows)
    c0 = 2 * D_CONV + D_POOL
    nblk = seq // tm
    assert tm == _attn_block(seq) and rows % seq == 0
    row_spec = lambda w: pl.BlockSpec((tm, w), lambda i: (i, 0))
    r_in, r_out, r_shapes = _rider_specs(riders, layer, rows // tm)
    outs = pl.pallas_call(
        functools.partial(_ffn1_win_kernel, len(riders)),
        grid=(rows // tm,),
        in_specs=[row_spec(d), _resident((1, d)), _resident((d, f)), _resident((d, f)),
                  _resident((f, d)), _resident((1, d)), _resident(win.shape),
                  _resident(gq.shape), _resident(gk.shape)] + r_in,
        out_specs=[row_spec(d), row_spec(c0),
                   pl.BlockSpec((None, N_HEADS, 2, V_DIM, tm), lambda i: (i // nblk, 0, 0, 0, i % nblk)),
                   pl.BlockSpec((None, N_HEADS, 2, tm, V_DIM), lambda i: (i // nblk, 0, 0, i % nblk, 0)),
                   pl.BlockSpec((None, N_HEADS, None, V_ROWS, tm),
                                lambda i: (i // nblk, 0, i % nblk, 0, 0))] + r_out,
        out_shape=[jax.ShapeDtypeStruct((rows, d), F32),
                   jax.ShapeDtypeStruct((rows, c0), F32),
                   jax.ShapeDtypeStruct((rows // seq, N_HEADS, 2, V_DIM, seq), BF16),
                   jax.ShapeDtypeStruct((rows // seq, N_HEADS, 2, seq, V_DIM), BF16),
                   jax.ShapeDtypeStruct((rows // seq, N_HEADS, nblk, V_ROWS, tm), BF16)] + r_shapes,
        compiler_params=pltpu.CompilerParams(dimension_semantics=("arbitrary",),
                                             vmem_limit_bytes=VMEM_LIMIT_BYTES),
        name="ffn1_win",
    )(x, g1, wg, wu, wd, gm, win, gq, gk, *riders)
    return (*outs[:5], list(outs[5:]))


def _convpool_tile(t0, seq, main_ref, prev_ref, next_ref, params, scratch, out_ref):
    dw_ref, dwb_ref, lng_ref, lnb_ref, pw_ref, ps_ref = params
    zext, zrot, pext, a1, a2, a3 = scratch
    ts = main_ref.shape[0]
    has_prev = t0 > 0
    has_next = t0 + ts < seq

    def glu(u):
        return u[:, :D_CONV] * _sigmoid(u[:, D_CONV:2 * D_CONV])

    def assemble():
        um = main_ref[...]
        up = prev_ref[...]
        un = next_ref[...]
        zero_halo = jnp.zeros((HALO, D_CONV), F32)
        zext[0:HALO, :] = jnp.where(has_prev, glu(up), zero_halo)
        zext[HALO:HALO + ts, :] = glu(um)
        zext[HALO + ts:, :] = jnp.where(has_next, glu(un), zero_halo)
        pext[0:HALO, :] = jnp.where(has_prev, up[:, 2 * D_CONV:], zero_halo)
        pext[HALO:HALO + ts, :] = um[:, 2 * D_CONV:]
        pext[HALO + ts:, :] = jnp.where(has_next, un[:, 2 * D_CONV:], zero_halo)

    def conv():
        off0 = HALO - CONV_WIDTH // 2
        nz = ts + 2 * HALO - SUBLANES
        for r in range(1, SUBLANES):
            zrot[r - 1, 0:nz, :] = zext[r:r + nz, :]
        acc = jnp.zeros((ts, D_CONV), F32)
        for j in range(CONV_WIDTH):
            r, a = (off0 + j) % SUBLANES, (off0 + j) // SUBLANES * SUBLANES
            tap = zext[a:a + ts, :] if r == 0 else zrot[r - 1, a:a + ts, :]
            acc = acc + dw_ref[j:j + 1, :] * tap
        z = acc + dwb_ref[...]
        mu = jnp.mean(z, axis=-1, keepdims=True)
        zc = z - mu
        var = jnp.mean(zc * zc, axis=-1, keepdims=True)
        y = zc * lax.rsqrt(var + NORM_EPS) * lng_ref[...] + lnb_ref[...]
        out_ref[:, :D_CONV] = (y * _sigmoid(y)).astype(out_ref.dtype)

    def pool():
        n = ts + 2 * HALO
        a1[1:n, :] = pext[0:n - 1, :] + pext[1:n, :]
        a2[2:n - 1, :] = a1[1:n - 2, :] + a1[3:n, :]
        a3[4:n - 3, :] = a2[2:n - 5, :] + a2[6:n - 1, :]
        w2 = a1[HALO:HALO + ts, :]
        w4 = a2[HALO:HALO + ts, :]
        w8 = a3[HALO:HALO + ts, :]
        w16 = a3[HALO - 4:HALO - 4 + ts, :] + a3[HALO + 4:HALO + 4 + ts, :]
        grp = lax.broadcasted_iota(jnp.int32, (ts, D_POOL), 1) // POOL_GROUP_DIM
        win = jnp.where(grp == 0, w2, jnp.where(grp == 1, w4, jnp.where(grp == 2, w8, w16)))
        t = lax.broadcasted_iota(jnp.int32, (ts, D_POOL), 0) + t0
        half = jnp.left_shift(1, grp)
        cnt = jnp.minimum(t + half, seq) - jnp.maximum(t - half, 0)
        d = win / cnt.astype(F32) - pext[HALO:HALO + ts, :]
        yp = jnp.dot(d.astype(BF16), pw_ref[...], preferred_element_type=F32) * ps_ref[...]
        out_ref[:, D_CONV:] = yp.astype(out_ref.dtype)

    return [assemble, conv, pool]


def _convpool_kernel(seq, main_ref, prev_ref, next_ref, dw_ref, dwb_ref, lng_ref, lnb_ref,
                     pw_ref, ps_ref, out_ref, *scratch):
    t0 = pl.program_id(1) * main_ref.shape[0]
    for stage in _convpool_tile(t0, seq, main_ref, prev_ref, next_ref,
                                (dw_ref, dwb_ref, lng_ref, lnb_ref, pw_ref, ps_ref), scratch, out_ref):
        stage()


def _convpool(ucp, dw, dwb, lng, lnb, pw_bd, ps):
    b, seq, c0 = ucp.shape
    ts = _attn_block(seq)
    nh = ts // HALO
    last_halo = seq // HALO - 1
    n = ts + 2 * HALO
    return pl.pallas_call(
        functools.partial(_convpool_kernel, seq),
        grid=(b, seq // ts),
        in_specs=[pl.BlockSpec((None, ts, c0), lambda bi, i: (bi, i, 0)),
                  pl.BlockSpec((None, HALO, c0), lambda bi, i: (bi, jnp.maximum(i * nh - 1, 0), 0)),
                  pl.BlockSpec((None, HALO, c0),
                               lambda bi, i: (bi, jnp.minimum((i + 1) * nh, last_halo), 0)),
                  _resident(dw.shape), _resident(dwb.shape), _resident(lng.shape),
                  _resident(lnb.shape), _resident(pw_bd.shape), _resident(ps.shape)],
        out_specs=pl.BlockSpec((None, ts, D_CONV + D_POOL), lambda bi, i: (bi, i, 0)),
        out_shape=jax.ShapeDtypeStruct((b, seq, D_CONV + D_POOL), BF16),
        scratch_shapes=[pltpu.VMEM((n, D_CONV), F32), pltpu.VMEM((SUBLANES - 1, n, D_CONV), F32),
                        pltpu.VMEM((n, D_POOL), F32), pltpu.VMEM((n, D_POOL), F32),
                        pltpu.VMEM((n, D_POOL), F32), pltpu.VMEM((n, D_POOL), F32)],
        compiler_params=pltpu.CompilerParams(dimension_semantics=("arbitrary", "arbitrary"),
                                             vmem_limit_bytes=VMEM_LIMIT_BYTES),
        name="convpool",
    )(ucp, ucp, ucp, dw, dwb, lng, lnb, pw_bd, ps)


def _attn_kernel(lambda_init, unshifted, slopes_ref, qT_ref, ka_ref, vT_ref, lq1_ref, lk1_ref,
                 lq2_ref, lk2_ref, sub_ref, out_ref, qv_ref, m_ref, acc_ref):
    h = pl.program_id(1)
    i = pl.program_id(2)
    nk = ka_ref.shape[1]
    tk = ka_ref.shape[2]
    tq = qT_ref.shape[2]
    slope = slopes_ref[h]

    row = lax.broadcasted_iota(jnp.int32, (V_DIM, tq), 0)
    for c in range(2):
        q = qT_ref[c]
        qv_ref[0, c] = jnp.where(row < HEAD_DIM, q, -q)
        qv_ref[1, c] = jnp.where(row < HEAD_DIM, q, jnp.zeros_like(q))
        qv_ref[2, c] = q
    acc_ref[...] = jnp.zeros(acc_ref.shape, F32)

    def diag_unit():
        kk = lax.broadcasted_iota(jnp.int32, (tk, tq), 0)
        qq = lax.broadcasted_iota(jnp.int32, (tk, tq), 1)
        return i, 1, slope * jnp.abs(kk - qq).astype(F32)

    def off_diag_unit(t):
        j = jnp.where(t >= i, t + 1, t)
        dist = jnp.abs(j - i) * tk
        return j, jnp.where(j > i, 2, 0), slope * jnp.full((1, tq), dist, jnp.int32).astype(F32)

    def scores(unit, c):
        j, side, bias = unit
        return jnp.dot(ka_ref[c, j], qv_ref[side, c], preferred_element_type=F32) - bias

    def unshifted_chunk(units):
        chains = [(u, c) for u in units for c in range(2)]
        probs = [None] * len(chains)
        pv = [[], []]

        def score_stage(k):
            probs[k] = jnp.exp(scores(*chains[k])).astype(BF16)

        def pv_stage(k):
            (j, _, _), c = chains[k]
            pv[c].append(jnp.dot(vT_ref[j], probs[k], preferred_element_type=F32))

        score_stage(0)
        for k in range(1, len(chains)):
            score_stage(k)
            pv_stage(k - 1)
        pv_stage(len(chains) - 1)
        for c in range(2):
            acc_ref[c] += functools.reduce(lambda a, b: a + b, pv[c])

    def running_max_step(unit):
        for c in range(2):
            s = scores(unit, c)
            m_old = m_ref[c]
            m_new = jnp.maximum(m_old, jnp.max(s, axis=0, keepdims=True))
            p = jnp.exp(s - m_new).astype(BF16)
            alpha = jnp.exp(m_old - m_new)
            acc_ref[c] = acc_ref[c] * alpha + jnp.dot(vT_ref[unit[0]], p, preferred_element_type=F32)
            m_ref[c] = m_new

    if unshifted:
        chunk = min(UNSHIFTED_CHUNK, nk)
        unshifted_chunk([diag_unit()] + [off_diag_unit(t) for t in range(chunk - 1)])

        def body(g, carry):
            t0 = chunk - 1 + g * chunk
            unshifted_chunk([off_diag_unit(t0 + r) for r in range(chunk)])
            return carry

        lax.fori_loop(0, nk // chunk - 1, body, 0)
    else:
        m_ref[...] = jnp.full(m_ref.shape, NEG_BIG, F32)
        running_max_step(diag_unit())

        def body(t, carry):
            running_max_step(off_diag_unit(t))
            return carry

        lax.fori_loop(0, nk - 1, body, 0)

    lam = (jnp.exp(jnp.sum(lq1_ref[...] * lk1_ref[...], axis=-1, keepdims=True))
           - jnp.exp(jnp.sum(lq2_ref[...] * lk2_ref[...], axis=-1, keepdims=True))
           + lambda_init)
    a0 = acc_ref[0]
    a1 = acc_ref[1]
    o = a0[:V_DIM] / a0[V_DIM:V_DIM + 1] - lam * (a1[:V_DIM] / a1[V_DIM:V_DIM + 1])
    ms = jnp.mean(o * o, axis=0, keepdims=True)
    y = o * lax.rsqrt(ms + NORM_EPS) * sub_ref[...] * (1.0 - lambda_init)
    out_ref[...] = y.T.astype(out_ref.dtype)


def _attention(qT, ka, vT, lq1, lk1, lq2, lk2, sub_col, lambda_init, unshifted):
    b, nh, _, kw, seq = qT.shape
    blk = vT.shape[-1]
    nblk = seq // blk
    assert kw == V_DIM and nblk % min(UNSHIFTED_CHUNK, nblk) == 0
    ka = ka.reshape(b, nh, 2, nblk, blk, kw)
    slopes = jnp.asarray([_slope(h) for h in range(nh)], F32)
    return pl.pallas_call(
        functools.partial(_attn_kernel, lambda_init, unshifted),
        grid=(b, nh, nblk),
        in_specs=[pl.BlockSpec(memory_space=pltpu.SMEM),
                  pl.BlockSpec((None, None, 2, kw, blk), lambda bi, h, i: (bi, h, 0, 0, i)),
                  pl.BlockSpec((None, None, 2, nblk, blk, kw), lambda bi, h, i: (bi, h, 0, 0, 0, 0)),
                  pl.BlockSpec((None, None, nblk, V_ROWS, blk), lambda bi, h, i: (bi, h, 0, 0, 0)),
                  _resident(lq1.shape), _resident(lk1.shape), _resident(lq2.shape),
                  _resident(lk2.shape), _resident(sub_col.shape)],
        out_specs=pl.BlockSpec((None, blk, V_DIM), lambda bi, h, i: (bi, i, h)),
        out_shape=jax.ShapeDtypeStruct((b, seq, nh * V_DIM), BF16),
        scratch_shapes=[pltpu.VMEM((3, 2, kw, blk), qT.dtype),
                        pltpu.VMEM((2, 1, blk), F32),
                        pltpu.VMEM((2, V_ROWS, blk), F32)],
        compiler_params=pltpu.CompilerParams(
            dimension_semantics=("arbitrary", "arbitrary", "arbitrary"),
            vmem_limit_bytes=VMEM_LIMIT_BYTES),
        name="attention_unshifted" if unshifted else "attention_running_max",
    )(slopes, qT, ka, vT, lq1, lk1, lq2, lk2, sub_col)


def _diff_attention(qT, ka, vT, q_gain, k_gain, lq1, lk1, lq2, lk2, sub_col, lambda_init):
    score_bound = (1.01 * HEAD_DIM ** 0.5) * jnp.max(jnp.abs(q_gain)) * jnp.max(jnp.abs(k_gain))

    def branch(unshifted):
        return lambda qT, ka, vT: _attention(qT, ka, vT, lq1, lk1, lq2, lk2, sub_col, lambda_init,
                                             unshifted)

    return lax.cond(score_bound <= MAX_UNSHIFTED_SCORE, branch(True), branch(False), qT, ka, vT)


def _out_ffn2_kernel(n_riders, x1_ref, ycp_ref, yat_ref, wo_ref, g2_ref, wg_ref, wu_ref,
                     wd_ref, gp_ref, *refs):
    d_cp = ycp_ref.shape[1]
    out_ref = refs[n_riders]
    _cast_riders(refs[:n_riders], refs[n_riders + 1:])
    halves = _row_halves(x1_ref.shape[0])
    x2s = [x1_ref[r, :]
           + jnp.dot(ycp_ref[r, :], wo_ref[:d_cp, :], preferred_element_type=F32)
           + jnp.dot(yat_ref[r, :], wo_ref[d_cp:, :], preferred_element_type=F32) for r in halves]
    ys = _swiglu_halves([_rms(x2, g2_ref[...]) for x2 in x2s], wg_ref, wu_ref, wd_ref)
    for r, x2, y in zip(halves, x2s, ys):
        out_ref[r, :] = _rms(x2 + 0.5 * y, gp_ref[...])


def _out_ffn2(x1, ycp, yat, wo, g2, wg, wu, wd, gp, riders, layer):
    rows, d = x1.shape
    f = wg.shape[1]
    tm = _row_block(rows)
    row_spec = lambda w: pl.BlockSpec((tm, w), lambda i: (i, 0))
    r_in, r_out, r_shapes = _rider_specs(riders, layer, rows // tm)
    outs = pl.pallas_call(
        functools.partial(_out_ffn2_kernel, len(riders)),
        grid=(rows // tm,),
        in_specs=[row_spec(d), row_spec(ycp.shape[1]), row_spec(yat.shape[1]),
                  _resident(wo.shape), _resident((1, d)),
                  _resident((d, f)), _resident((d, f)), _resident((f, d)), _resident((1, d))] + r_in,
        out_specs=[row_spec(d)] + r_out,
        out_shape=[jax.ShapeDtypeStruct((rows, d), F32)] + r_shapes,
        compiler_params=pltpu.CompilerParams(dimension_semantics=("arbitrary",),
                                             vmem_limit_bytes=VMEM_LIMIT_BYTES),
        name="out_ffn2",
    )(x1, ycp, yat, wo, g2, wg, wu, wd, gp, *riders)
    return outs[0], list(outs[1:])


def _pool_block_diag(pool_w):
    g, cg, _ = pool_w.shape
    out = jnp.zeros((g * cg, g * cg), pool_w.dtype)
    for k in range(g):
        out = out.at[k * cg:(k + 1) * cg, k * cg:(k + 1) * cg].set(pool_w[k])
    return out


def kernel(x, ffn1_norm, ffn1_w_gate, ffn1_w_up, ffn1_w_down, mix_norm, w_in, conv_dw, conv_dw_bias, conv_ln_gain, conv_ln_bias, pool_w, pool_scale, q_norm, k_norm, lambda_q1, lambda_k1, lambda_q2, lambda_k2, attn_subln, w_out, ffn2_norm, ffn2_w_gate, ffn2_w_up, ffn2_w_down, post_norm):
    b, seq, d = x.shape
    depth = w_in.shape[0]
    rows = b * seq
    d_cp = D_CONV + D_POOL
    row = lambda v: v.reshape(1, -1).astype(F32)
    xf = x.reshape(rows, d).astype(F32)
    stage1 = [w[0].astype(BF16) for w in (ffn1_w_gate, ffn1_w_up, ffn1_w_down, w_in)]
    for l in range(depth):
        lambda_init = 0.8 - 0.6 * math.exp(-0.3 * l)
        pair = lambda g: jnp.concatenate([g, g]).reshape(1, -1).astype(F32)
        x1, ucp, qT, ka, vT, stage3 = _ffn1_win(
            xf, row(ffn1_norm[l]), stage1[0], stage1[1], stage1[2], row(mix_norm[l]), stage1[3],
            pair(q_norm[l]), pair(k_norm[l]), seq,
            [ffn2_w_gate, ffn2_w_up, ffn2_w_down, w_out], l)
        ycp = _convpool(
            ucp.reshape(b, seq, -1), conv_dw[l].astype(F32), row(conv_dw_bias[l]),
            row(conv_ln_gain[l]), row(conv_ln_bias[l]),
            _pool_block_diag(pool_w[l]).astype(BF16), row(pool_scale[l]))
        yat = _diff_attention(
            qT, ka, vT, q_norm[l], k_norm[l], row(lambda_q1[l]), row(lambda_k1[l]), row(lambda_q2[l]),
            row(lambda_k2[l]), attn_subln[l].reshape(-1, 1).astype(F32), lambda_init)
        nxt = l + 1
        xf, stage1 = _out_ffn2(
            x1, ycp.reshape(rows, d_cp), yat.reshape(rows, -1), stage3[3],
            row(ffn2_norm[l]), stage3[0], stage3[1], stage3[2], row(post_norm[l]),
            [ffn1_w_gate, ffn1_w_up, ffn1_w_down, w_in] if nxt < depth else [], nxt)
    return xf.reshape(b, seq, d).astype(x.dtype)
```

```python
import functools
import math

import jax
import jax.numpy as jnp
from jax import lax
from jax.experimental import pallas as pl
from jax.experimental.pallas import tpu as pltpu

F32 = jnp.float32
BF16 = jnp.bfloat16

NORM_EPS = 1e-6
N_HEADS = 4
HEAD_DIM = 64
V_DIM = 2 * HEAD_DIM
D_CONV = 256
D_POOL = 256
D_ATTN = 512
CONV_WIDTH = 31
POOL_GROUP_DIM = 64
SUBLANES = 8
HALO = 16
V_ROWS = V_DIM + 8
BF16_EXACT_INT = 256
VMEM_LIMIT_BYTES = 56 * 1024 * 1024
NEG_BIG = -1e30
MAX_UNSHIFTED_SCORE = 60.0
UNSHIFTED_CHUNK = 16


def _attn_block(seq):
    return min(512, seq)


def _row_block(rows):
    return min(512, rows)


def _rms(x, g):
    ms = jnp.mean(x * x, axis=-1, keepdims=True)
    return x * lax.rsqrt(ms + NORM_EPS) * g


def _sigmoid(x):
    return 1.0 / (1.0 + jnp.exp(-x))


def _row_halves(tm):
    half = tm // 2 if tm % (2 * SUBLANES) == 0 else tm
    return [slice(r, r + half) for r in range(0, tm, half)]


def _swiglu_halves(hs, wg_ref, wu_ref, wd_ref):
    hbs = [h.astype(BF16) for h in hs]
    gus = [(jnp.dot(hb, wg_ref[...], preferred_element_type=F32),
            jnp.dot(hb, wu_ref[...], preferred_element_type=F32)) for hb in hbs]
    acts = [(g * _sigmoid(g) * u).astype(BF16) for g, u in gus]
    return [jnp.dot(a, wd_ref[...], preferred_element_type=F32) for a in acts]


def _resident(shape):
    nd = len(shape)
    return pl.BlockSpec(shape, lambda *_: (0,) * nd, pipeline_mode=pl.Buffered(1))


def _slope(h):
    return 2.0 ** (-8.0 * (h + 1) / N_HEADS)


def _attention_operands(uq, uk, uv, rows, gq_ref, gk_ref, qT_ref, ka_ref, vT_ref):
    n = uq.shape[0]
    blk = qT_ref.shape[-1]
    lane = lax.broadcasted_iota(jnp.int32, (n, V_DIM), 1)
    low = lane < HEAD_DIM
    pos = lax.broadcasted_iota(jnp.int32, (n, V_DIM), 0) + rows.start
    one = jnp.ones((n, V_DIM), F32)
    zero = jnp.zeros((n, V_DIM), F32)
    row8 = lax.broadcasted_iota(jnp.int32, (V_ROWS - V_DIM, n), 0)
    ones_rows = jnp.where(row8 == 0, 1.0, 0.0).astype(vT_ref.dtype)
    digits = [(pos % BF16_EXACT_INT).astype(F32)]
    place = BF16_EXACT_INT
    while place < blk:
        digits.append((pos % (place * BF16_EXACT_INT) - pos % place).astype(F32))
        place *= BF16_EXACT_INT

    def norm_pair(x, g):
        x2 = x * x
        s_lo = jnp.sum(jnp.where(low, x2, 0.0), axis=-1, keepdims=True)
        s_hi = jnp.sum(jnp.where(low, 0.0, x2), axis=-1, keepdims=True)
        ms = jnp.where(low, s_lo, s_hi) * (1.0 / HEAD_DIM)
        return x * lax.rsqrt(ms + NORM_EPS) * g

    def aug(cols):
        a = zero
        for k, col in enumerate(cols):
            a = jnp.where(lane == HEAD_DIM + k, col, a)
        return a

    def other_map(x):
        return pltpu.roll(x, HEAD_DIM, 1)

    for h in range(N_HEADS):
        cols = slice(h * V_DIM, (h + 1) * V_DIM)
        qn = norm_pair(uq[:, cols], gq_ref[...]) * (HEAD_DIM ** -0.5)
        kn = norm_pair(uk[:, cols], gk_ref[...])
        coef = _slope(h)
        q_aug = aug([-coef * one] * len(digits) + digits)
        k_aug = aug(digits + [coef * one] * len(digits))
        for c in range(2):
            qc = qn if c == 0 else other_map(qn)
            kc = kn if c == 0 else other_map(kn)
            qT_ref[h, c, :, rows] = jnp.where(low, qc, q_aug).T.astype(qT_ref.dtype)
            ka_ref[h, c, rows, :] = jnp.where(low, kc, k_aug).astype(ka_ref.dtype)
        vT_ref[h, :V_DIM, rows] = uv[:, cols].T.astype(vT_ref.dtype)
        vT_ref[h, V_DIM:, rows] = ones_rows


def _rider_specs(stacked, layer, steps):
    in_specs, out_specs, out_shapes = [], [], []
    for w in stacked:
        _, r, c = w.shape
        rpb = next(k for k in range(2 * SUBLANES, r + 1, 2 * SUBLANES)
                   if r % k == 0 and r // k <= steps)
        last = r // rpb - 1
        in_specs.append(pl.BlockSpec((None, rpb, c), lambda i, last=last: (layer, jnp.minimum(i, last), 0)))
        out_specs.append(pl.BlockSpec((rpb, c), lambda i, last=last: (jnp.minimum(i, last), 0)))
        out_shapes.append(jax.ShapeDtypeStruct((r, c), BF16))
    return in_specs, out_specs, out_shapes


def _cast_riders(in_refs, out_refs):
    for src, dst in zip(in_refs, out_refs):
        dst[...] = src[...].astype(dst.dtype)


def _ffn1_win_kernel(n_riders, x_ref, g1_ref, wg_ref, wu_ref, wd_ref, gm_ref, win_ref, gq_ref, gk_ref,
                     *refs):
    x1_ref, ucp_ref, qT_ref, ka_ref, vT_ref = refs[n_riders:n_riders + 5]
    _cast_riders(refs[:n_riders], refs[n_riders + 5:])
    halves = _row_halves(x_ref.shape[0])
    xs = [x_ref[r, :] for r in halves]
    ys = _swiglu_halves([_rms(x, g1_ref[...]) for x in xs], wg_ref, wu_ref, wd_ref)
    x1s = [x + 0.5 * y for x, y in zip(xs, ys)]
    hms = [_rms(x1, gm_ref[...]).astype(BF16) for x1 in x1s]
    c0 = 2 * D_CONV + D_POOL
    u_attn = [jnp.dot(hm, win_ref[:, c0:], preferred_element_type=F32) for hm in hms]
    for r, u in zip(halves, u_attn):
        _attention_operands(u[:, :D_ATTN], u[:, D_ATTN:2 * D_ATTN], u[:, 2 * D_ATTN:], r,
                            gq_ref, gk_ref, qT_ref, ka_ref, vT_ref)
    for r, x1, hm in zip(halves, x1s, hms):
        x1_ref[r, :] = x1
        ucp_ref[r, :] = jnp.dot(hm, win_ref[:, :c0], preferred_element_type=F32)


def _ffn1_win(x, g1, wg, wu, wd, gm, win, gq, gk, seq, riders, layer):
    rows, d = x.shape
    f = wg.shape[1]
    tm = _row_block(rows)
    c0 = 2 * D_CONV + D_POOL
    nblk = seq // tm
    assert tm == _attn_block(seq) and rows % seq == 0
    row_spec = lambda w: pl.BlockSpec((tm, w), lambda i: (i, 0))
    r_in, r_out, r_shapes = _rider_specs(riders, layer, rows // tm)
    outs = pl.pallas_call(
        functools.partial(_ffn1_win_kernel, len(riders)),
        grid=(rows // tm,),
        in_specs=[row_spec(d), _resident((1, d)), _resident((d, f)), _resident((d, f)),
                  _resident((f, d)), _resident((1, d)), _resident(win.shape),
                  _resident(gq.shape), _resident(gk.shape)] + r_in,
        out_specs=[row_spec(d), row_spec(c0),
                   pl.BlockSpec((None, N_HEADS, 2, V_DIM, tm), lambda i: (i // nblk, 0, 0, 0, i % nblk)),
                   pl.BlockSpec((None, N_HEADS, 2, tm, V_DIM), lambda i: (i // nblk, 0, 0, i % nblk, 0)),
                   pl.BlockSpec((None, N_HEADS, None, V_ROWS, tm),
                                lambda i: (i // nblk, 0, i % nblk, 0, 0))] + r_out,
        out_shape=[jax.ShapeDtypeStruct((rows, d), F32),
                   jax.ShapeDtypeStruct((rows, c0), F32),
                   jax.ShapeDtypeStruct((rows // seq, N_HEADS, 2, V_DIM, seq), BF16),
                   jax.ShapeDtypeStruct((rows // seq, N_HEADS, 2, seq, V_DIM), BF16),
                   jax.ShapeDtypeStruct((rows // seq, N_HEADS, nblk, V_ROWS, tm), BF16)] + r_shapes,
        compiler_params=pltpu.CompilerParams(dimension_semantics=("arbitrary",),
                                             vmem_limit_bytes=VMEM_LIMIT_BYTES),
        name="ffn1_win",
    )(x, g1, wg, wu, wd, gm, win, gq, gk, *riders)
    return (*outs[:5], list(outs[5:]))


def _convpool_tile(t0, seq, main_ref, prev_ref, next_ref, params, scratch, out_ref):
    dw_ref, dwb_ref, lng_ref, lnb_ref, pw_ref, ps_ref = params
    zext, zrot, pext, a1, a2, a3 = scratch
    ts = main_ref.shape[0]
    has_prev = t0 > 0
    has_next = t0 + ts < seq

    def glu(u):
        return u[:, :D_CONV] * _sigmoid(u[:, D_CONV:2 * D_CONV])

    def assemble():
        um = main_ref[...]
        up = prev_ref[...]
        un = next_ref[...]
        zero_halo = jnp.zeros((HALO, D_CONV), F32)
        zext[0:HALO, :] = jnp.where(has_prev, glu(up), zero_halo)
        zext[HALO:HALO + ts, :] = glu(um)
        zext[HALO + ts:, :] = jnp.where(has_next, glu(un), zero_halo)
        pext[0:HALO, :] = jnp.where(has_prev, up[:, 2 * D_CONV:], zero_halo)
        pext[HALO:HALO + ts, :] = um[:, 2 * D_CONV:]
        pext[HALO + ts:, :] = jnp.where(has_next, un[:, 2 * D_CONV:], zero_halo)

    def conv():
        off0 = HALO - CONV_WIDTH // 2
        nz = ts + 2 * HALO - SUBLANES
        for r in range(1, SUBLANES):
            zrot[r - 1, 0:nz, :] = zext[r:r + nz, :]
        acc = jnp.zeros((ts, D_CONV), F32)
        for j in range(CONV_WIDTH):
            r, a = (off0 + j) % SUBLANES, (off0 + j) // SUBLANES * SUBLANES
            tap = zext[a:a + ts, :] if r == 0 else zrot[r - 1, a:a + ts, :]
            acc = acc + dw_ref[j:j + 1, :] * tap
        z = acc + dwb_ref[...]
        mu = jnp.mean(z, axis=-1, keepdims=True)
        zc = z - mu
        var = jnp.mean(zc * zc, axis=-1, keepdims=True)
        y = zc * lax.rsqrt(var + NORM_EPS) * lng_ref[...] + lnb_ref[...]
        out_ref[:, :D_CONV] = (y * _sigmoid(y)).astype(out_ref.dtype)

    def pool():
        n = ts + 2 * HALO
        a1[1:n, :] = pext[0:n - 1, :] + pext[1:n, :]
        a2[2:n - 1, :] = a1[1:n - 2, :] + a1[3:n, :]
        a3[4:n - 3, :] = a2[2:n - 5, :] + a2[6:n - 1, :]
        w2 = a1[HALO:HALO + ts, :]
        w4 = a2[HALO:HALO + ts, :]
        w8 = a3[HALO:HALO + ts, :]
        w16 = a3[HALO - 4:HALO - 4 + ts, :] + a3[HALO + 4:HALO + 4 + ts, :]
        grp = lax.broadcasted_iota(jnp.int32, (ts, D_POOL), 1) // POOL_GROUP_DIM
        win = jnp.where(grp == 0, w2, jnp.where(grp == 1, w4, jnp.where(grp == 2, w8, w16)))
        t = lax.broadcasted_iota(jnp.int32, (ts, D_POOL), 0) + t0
        half = jnp.left_shift(1, grp)
        cnt = jnp.minimum(t + half, seq) - jnp.maximum(t - half, 0)
        d = win / cnt.astype(F32) - pext[HALO:HALO + ts, :]
        yp = jnp.dot(d.astype(BF16), pw_ref[...], preferred_element_type=F32) * ps_ref[...]
        out_ref[:, D_CONV:] = yp.astype(out_ref.dtype)

    return [assemble, conv, pool]


def _convpool_kernel(seq, main_ref, prev_ref, next_ref, dw_ref, dwb_ref, lng_ref, lnb_ref,
                     pw_ref, ps_ref, out_ref, *scratch):
    t0 = pl.program_id(1) * main_ref.shape[0]
    for stage in _convpool_tile(t0, seq, main_ref, prev_ref, next_ref,
                                (dw_ref, dwb_ref, lng_ref, lnb_ref, pw_ref, ps_ref), scratch, out_ref):
        stage()


def _convpool(ucp, dw, dwb, lng, lnb, pw_bd, ps):
    b, seq, c0 = ucp.shape
    ts = _attn_block(seq)
    nh = ts // HALO
    last_halo = seq // HALO - 1
    n = ts + 2 * HALO
    return pl.pallas_call(
        functools.partial(_convpool_kernel, seq),
        grid=(b, seq // ts),
        in_specs=[pl.BlockSpec((None, ts, c0), lambda bi, i: (bi, i, 0)),
                  pl.BlockSpec((None, HALO, c0), lambda bi, i: (bi, jnp.maximum(i * nh - 1, 0), 0)),
                  pl.BlockSpec((None, HALO, c0),
                               lambda bi, i: (bi, jnp.minimum((i + 1) * nh, last_halo), 0)),
                  _resident(dw.shape), _resident(dwb.shape), _resident(lng.shape),
                  _resident(lnb.shape), _resident(pw_bd.shape), _resident(ps.shape)],
        out_specs=pl.BlockSpec((None, ts, D_CONV + D_POOL), lambda bi, i: (bi, i, 0)),
        out_shape=jax.ShapeDtypeStruct((b, seq, D_CONV + D_POOL), BF16),
        scratch_shapes=[pltpu.VMEM((n, D_CONV), F32), pltpu.VMEM((SUBLANES - 1, n, D_CONV), F32),
                        pltpu.VMEM((n, D_POOL), F32), pltpu.VMEM((n, D_POOL), F32),
                        pltpu.VMEM((n, D_POOL), F32), pltpu.VMEM((n, D_POOL), F32)],
        compiler_params=pltpu.CompilerParams(dimension_semantics=("arbitrary", "arbitrary"),
                                             vmem_limit_bytes=VMEM_LIMIT_BYTES),
        name="convpool",
    )(ucp, ucp, ucp, dw, dwb, lng, lnb, pw_bd, ps)


def _attn_kernel(lambda_init, unshifted, n_blocks, slopes_ref, qT_ref, ka_ref, vT_ref, lq1_ref, lk1_ref,
                 lq2_ref, lk2_ref, sub_ref, out_ref, qv_ref, m_ref, acc2_ref):
    f = pl.program_id(0)
    nk = ka_ref.shape[1]
    tk = ka_ref.shape[2]
    tq = qT_ref.shape[2]
    block = jnp.minimum(f, n_blocks - 1)
    i = block % nk
    slope = slopes_ref[(block // nk) % N_HEADS]
    acc_ref = acc2_ref.at[f % 2]

    @pl.when(f == 0)
    def _():
        acc2_ref[1] = jnp.ones(acc2_ref.shape[1:], F32)

    lam = (jnp.exp(jnp.sum(lq1_ref[...] * lk1_ref[...], axis=-1, keepdims=True))
           - jnp.exp(jnp.sum(lq2_ref[...] * lk2_ref[...], axis=-1, keepdims=True))
           + lambda_init)
    a0 = acc2_ref[1 - f % 2, 0]
    a1 = acc2_ref[1 - f % 2, 1]
    o = a0[:V_DIM] / a0[V_DIM:V_DIM + 1] - lam * (a1[:V_DIM] / a1[V_DIM:V_DIM + 1])
    ms = jnp.mean(o * o, axis=0, keepdims=True)
    y = o * lax.rsqrt(ms + NORM_EPS) * sub_ref[...] * (1.0 - lambda_init)
    out_ref[...] = y.T.astype(out_ref.dtype)

    row = lax.broadcasted_iota(jnp.int32, (V_DIM, tq), 0)
    for c in range(2):
        q = qT_ref[c]
        qv_ref[0, c] = jnp.where(row < HEAD_DIM, q, -q)
        qv_ref[1, c] = jnp.where(row < HEAD_DIM, q, jnp.zeros_like(q))
        qv_ref[2, c] = q
    acc_ref[...] = jnp.zeros(acc_ref.shape, F32)

    def diag_unit():
        kk = lax.broadcasted_iota(jnp.int32, (tk, tq), 0)
        qq = lax.broadcasted_iota(jnp.int32, (tk, tq), 1)
        return i, 1, slope * jnp.abs(kk - qq).astype(F32)

    def off_diag_unit(t):
        j = jnp.where(t >= i, t + 1, t)
        dist = jnp.abs(j - i) * tk
        return j, jnp.where(j > i, 2, 0), slope * jnp.full((1, tq), dist, jnp.int32).astype(F32)

    def scores(unit, c):
        j, side, bias = unit
        return jnp.dot(ka_ref[c, j], qv_ref[side, c], preferred_element_type=F32) - bias

    def unshifted_chunk(units):
        chains = [(u, c) for u in units for c in range(2)]
        probs = [None] * len(chains)
        pv = [[], []]

        def score_stage(k):
            probs[k] = jnp.exp(scores(*chains[k])).astype(BF16)

        def pv_stage(k):
            (j, _, _), c = chains[k]
            pv[c].append(jnp.dot(vT_ref[j], probs[k], preferred_element_type=F32))

        score_stage(0)
        for k in range(1, len(chains)):
            score_stage(k)
            pv_stage(k - 1)
        pv_stage(len(chains) - 1)
        for c in range(2):
            acc_ref[c] += functools.reduce(lambda a, b: a + b, pv[c])

    def running_max_step(unit):
        for c in range(2):
            s = scores(unit, c)
            m_old = m_ref[c]
            m_new = jnp.maximum(m_old, jnp.max(s, axis=0, keepdims=True))
            p = jnp.exp(s - m_new).astype(BF16)
            alpha = jnp.exp(m_old - m_new)
            acc_ref[c] = acc_ref[c] * alpha + jnp.dot(vT_ref[unit[0]], p, preferred_element_type=F32)
            m_ref[c] = m_new

    if unshifted:
        chunk = min(UNSHIFTED_CHUNK, nk)
        unshifted_chunk([diag_unit()] + [off_diag_unit(t) for t in range(chunk - 1)])

        def body(g, carry):
            t0 = chunk - 1 + g * chunk
            unshifted_chunk([off_diag_unit(t0 + r) for r in range(chunk)])
            return carry

        lax.fori_loop(0, nk // chunk - 1, body, 0)
    else:
        m_ref[...] = jnp.full(m_ref.shape, NEG_BIG, F32)
        running_max_step(diag_unit())

        def body(t, carry):
            running_max_step(off_diag_unit(t))
            return carry

        lax.fori_loop(0, nk - 1, body, 0)


def _attention(qT, ka, vT, lq1, lk1, lq2, lk2, sub_col, lambda_init, unshifted):
    b, nh, _, kw, seq = qT.shape
    blk = vT.shape[-1]
    nblk = seq // blk
    assert kw == V_DIM and nh == N_HEADS and nblk % min(UNSHIFTED_CHUNK, nblk) == 0
    ka = ka.reshape(b, nh, 2, nblk, blk, kw)
    slopes = jnp.asarray([_slope(h) for h in range(nh)], F32)
    n_blocks = b * nh * nblk

    def where(f):
        return f // (nh * nblk), (f // nblk) % nh, f % nblk

    def computed(f):
        return where(jnp.minimum(f, n_blocks - 1))

    def written(f):
        return where(jnp.maximum(f - 1, 0))

    return pl.pallas_call(
        functools.partial(_attn_kernel, lambda_init, unshifted, n_blocks),
        grid=(n_blocks + 1,),
        in_specs=[pl.BlockSpec(memory_space=pltpu.SMEM),
                  pl.BlockSpec((None, None, 2, kw, blk),
                               lambda f: (computed(f)[0], computed(f)[1], 0, 0, computed(f)[2])),
                  pl.BlockSpec((None, None, 2, nblk, blk, kw),
                               lambda f: (computed(f)[0], computed(f)[1], 0, 0, 0, 0)),
                  pl.BlockSpec((None, None, nblk, V_ROWS, blk),
                               lambda f: (computed(f)[0], computed(f)[1], 0, 0, 0)),
                  _resident(lq1.shape), _resident(lk1.shape), _resident(lq2.shape),
                  _resident(lk2.shape), _resident(sub_col.shape)],
        out_specs=pl.BlockSpec((None, blk, V_DIM),
                               lambda f: (written(f)[0], written(f)[2], written(f)[1])),
        out_shape=jax.ShapeDtypeStruct((b, seq, nh * V_DIM), BF16),
        scratch_shapes=[pltpu.VMEM((3, 2, kw, blk), qT.dtype),
                        pltpu.VMEM((2, 1, blk), F32),
                        pltpu.VMEM((2, 2, V_ROWS, blk), F32)],
        compiler_params=pltpu.CompilerParams(
            dimension_semantics=("arbitrary",),
            vmem_limit_bytes=VMEM_LIMIT_BYTES),
        name="attention_unshifted" if unshifted else "attention_running_max",
    )(slopes, qT, ka, vT, lq1, lk1, lq2, lk2, sub_col)


def _diff_attention(qT, ka, vT, q_gain, k_gain, lq1, lk1, lq2, lk2, sub_col, lambda_init):
    score_bound = (1.01 * HEAD_DIM ** 0.5) * jnp.max(jnp.abs(q_gain)) * jnp.max(jnp.abs(k_gain))

    def branch(unshifted):
        return lambda qT, ka, vT: _attention(qT, ka, vT, lq1, lk1, lq2, lk2, sub_col, lambda_init,
                                             unshifted)

    return lax.cond(score_bound <= MAX_UNSHIFTED_SCORE, branch(True), branch(False), qT, ka, vT)


def _out_ffn2_kernel(n_riders, x1_ref, ycp_ref, yat_ref, wo_ref, g2_ref, wg_ref, wu_ref,
                     wd_ref, gp_ref, *refs):
    d_cp = ycp_ref.shape[1]
    out_ref = refs[n_riders]
    _cast_riders(refs[:n_riders], refs[n_riders + 1:])
    halves = _row_halves(x1_ref.shape[0])
    x2s = [x1_ref[r, :]
           + jnp.dot(ycp_ref[r, :], wo_ref[:d_cp, :], preferred_element_type=F32)
           + jnp.dot(yat_ref[r, :], wo_ref[d_cp:, :], preferred_element_type=F32) for r in halves]
    ys = _swiglu_halves([_rms(x2, g2_ref[...]) for x2 in x2s], wg_ref, wu_ref, wd_ref)
    for r, x2, y in zip(halves, x2s, ys):
        out_ref[r, :] = _rms(x2 + 0.5 * y, gp_ref[...])


def _out_ffn2(x1, ycp, yat, wo, g2, wg, wu, wd, gp, riders, layer):
    rows, d = x1.shape
    f = wg.shape[1]
    tm = _row_block(rows)
    row_spec = lambda w: pl.BlockSpec((tm, w), lambda i: (i, 0))
    r_in, r_out, r_shapes = _rider_specs(riders, layer, rows // tm)
    outs = pl.pallas_call(
        functools.partial(_out_ffn2_kernel, len(riders)),
        grid=(rows // tm,),
        in_specs=[row_spec(d), row_spec(ycp.shape[1]), row_spec(yat.shape[1]),
                  _resident(wo.shape), _resident((1, d)),
                  _resident((d, f)), _resident((d, f)), _resident((f, d)), _resident((1, d))] + r_in,
        out_specs=[row_spec(d)] + r_out,
        out_shape=[jax.ShapeDtypeStruct((rows, d), F32)] + r_shapes,
        compiler_params=pltpu.CompilerParams(dimension_semantics=("arbitrary",),
                                             vmem_limit_bytes=VMEM_LIMIT_BYTES),
        name="out_ffn2",
    )(x1, ycp, yat, wo, g2, wg, wu, wd, gp, *riders)
    return outs[0], list(outs[1:])


def _pool_block_diag(pool_w):
    g, cg, _ = pool_w.shape
    out = jnp.zeros((g * cg, g * cg), pool_w.dtype)
    for k in range(g):
        out = out.at[k * cg:(k + 1) * cg, k * cg:(k + 1) * cg].set(pool_w[k])
    return out


def kernel(x, ffn1_norm, ffn1_w_gate, ffn1_w_up, ffn1_w_down, mix_norm, w_in, conv_dw, conv_dw_bias, conv_ln_gain, conv_ln_bias, pool_w, pool_scale, q_norm, k_norm, lambda_q1, lambda_k1, lambda_q2, lambda_k2, attn_subln, w_out, ffn2_norm, ffn2_w_gate, ffn2_w_up, ffn2_w_down, post_norm):
    b, seq, d = x.shape
    depth = w_in.shape[0]
    rows = b * seq
    d_cp = D_CONV + D_POOL
    row = lambda v: v.reshape(1, -1).astype(F32)
    xf = x.reshape(rows, d).astype(F32)
    stage1 = [w[0].astype(BF16) for w in (ffn1_w_gate, ffn1_w_up, ffn1_w_down, w_in)]
    for l in range(depth):
        lambda_init = 0.8 - 0.6 * math.exp(-0.3 * l)
        pair = lambda g: jnp.concatenate([g, g]).reshape(1, -1).astype(F32)
        x1, ucp, qT, ka, vT, stage3 = _ffn1_win(
            xf, row(ffn1_norm[l]), stage1[0], stage1[1], stage1[2], row(mix_norm[l]), stage1[3],
            pair(q_norm[l]), pair(k_norm[l]), seq,
            [ffn2_w_gate, ffn2_w_up, ffn2_w_down, w_out], l)
        ycp = _convpool(
            ucp.reshape(b, seq, -1), conv_dw[l].astype(F32), row(conv_dw_bias[l]),
            row(conv_ln_gain[l]), row(conv_ln_bias[l]),
            _pool_block_diag(pool_w[l]).astype(BF16), row(pool_scale[l]))
        yat = _diff_attention(
            qT, ka, vT, q_norm[l], k_norm[l], row(lambda_q1[l]), row(lambda_k1[l]), row(lambda_q2[l]),
            row(lambda_k2[l]), attn_subln[l].reshape(-1, 1).astype(F32), lambda_init)
        nxt = l + 1
        xf, stage1 = _out_ffn2(
            x1, ycp.reshape(rows, d_cp), yat.reshape(rows, -1), stage3[3],
            row(ffn2_norm[l]), stage3[0], stage3[1], stage3[2], row(post_norm[l]),
            [ffn1_w_gate, ffn1_w_up, ffn1_w_down, w_in] if nxt < depth else [], nxt)
    return xf.reshape(b, seq, d).astype(x.dtype)
```

```python
import functools
import math

import jax
import jax.numpy as jnp
from jax import lax
from jax.experimental import pallas as pl
from jax.experimental.pallas import tpu as pltpu

F32 = jnp.float32
BF16 = jnp.bfloat16

NORM_EPS = 1e-6
N_HEADS = 4
HEAD_DIM = 64
V_DIM = 2 * HEAD_DIM
D_CONV = 256
D_POOL = 256
D_ATTN = 512
CONV_WIDTH = 31
POOL_GROUP_DIM = 64
SUBLANES = 8
HALO = 16
V_ROWS = V_DIM + 8
BF16_EXACT_INT = 256
VMEM_LIMIT_BYTES = 56 * 1024 * 1024
NEG_BIG = -1e30
MAX_UNSHIFTED_SCORE = 60.0
UNSHIFTED_CHUNK = 16


def _attn_block(seq):
    return min(512, seq)


def _row_block(rows):
    return min(512, rows)


def _rms(x, g):
    ms = jnp.mean(x * x, axis=-1, keepdims=True)
    return x * lax.rsqrt(ms + NORM_EPS) * g


def _sigmoid(x):
    return 1.0 / (1.0 + jnp.exp(-x))


def _row_halves(tm):
    half = tm // 2 if tm % (2 * SUBLANES) == 0 else tm
    return [slice(r, r + half) for r in range(0, tm, half)]


def _swiglu_halves(hs, wg_ref, wu_ref, wd_ref):
    hbs = [h.astype(BF16) for h in hs]
    gus = [(jnp.dot(hb, wg_ref[...], preferred_element_type=F32),
            jnp.dot(hb, wu_ref[...], preferred_element_type=F32)) for hb in hbs]
    acts = [(g * _sigmoid(g) * u).astype(BF16) for g, u in gus]
    return [jnp.dot(a, wd_ref[...], preferred_element_type=F32) for a in acts]


def _resident(shape):
    nd = len(shape)
    return pl.BlockSpec(shape, lambda *_: (0,) * nd, pipeline_mode=pl.Buffered(1))


def _slope(h):
    return 2.0 ** (-8.0 * (h + 1) / N_HEADS)


def _attention_operands(uq, uk, uv, rows, gq_ref, gk_ref, qT_ref, ka_ref, vT_ref):
    n = uq.shape[0]
    blk = qT_ref.shape[-1]
    lane = lax.broadcasted_iota(jnp.int32, (n, V_DIM), 1)
    low = lane < HEAD_DIM
    pos = lax.broadcasted_iota(jnp.int32, (n, V_DIM), 0) + rows.start
    one = jnp.ones((n, V_DIM), F32)
    zero = jnp.zeros((n, V_DIM), F32)
    row8 = lax.broadcasted_iota(jnp.int32, (V_ROWS - V_DIM, n), 0)
    ones_rows = jnp.where(row8 == 0, 1.0, 0.0).astype(vT_ref.dtype)
    digits = [(pos % BF16_EXACT_INT).astype(F32)]
    place = BF16_EXACT_INT
    while place < blk:
        digits.append((pos % (place * BF16_EXACT_INT) - pos % place).astype(F32))
        place *= BF16_EXACT_INT

    def norm_pair(x, g):
        x2 = x * x
        s_lo = jnp.sum(jnp.where(low, x2, 0.0), axis=-1, keepdims=True)
        s_hi = jnp.sum(jnp.where(low, 0.0, x2), axis=-1, keepdims=True)
        ms = jnp.where(low, s_lo, s_hi) * (1.0 / HEAD_DIM)
        return x * lax.rsqrt(ms + NORM_EPS) * g

    def aug(cols):
        a = zero
        for k, col in enumerate(cols):
            a = jnp.where(lane == HEAD_DIM + k, col, a)
        return a

    def other_map(x):
        return pltpu.roll(x, HEAD_DIM, 1)

    for h in range(N_HEADS):
        cols = slice(h * V_DIM, (h + 1) * V_DIM)
        qn = norm_pair(uq[:, cols], gq_ref[...]) * (HEAD_DIM ** -0.5)
        kn = norm_pair(uk[:, cols], gk_ref[...])
        coef = _slope(h)
        q_aug = aug([-coef * one] * len(digits) + digits)
        k_aug = aug(digits + [coef * one] * len(digits))
        for c in range(2):
            qc = qn if c == 0 else other_map(qn)
            kc = kn if c == 0 else other_map(kn)
            qT_ref[h, c, :, rows] = jnp.where(low, qc, q_aug).T.astype(qT_ref.dtype)
            ka_ref[h, c, rows, :] = jnp.where(low, kc, k_aug).astype(ka_ref.dtype)
        vT_ref[h, :V_DIM, rows] = uv[:, cols].T.astype(vT_ref.dtype)
        vT_ref[h, V_DIM:, rows] = ones_rows


def _rider_specs(stacked, layer, steps):
    in_specs, out_specs, out_shapes = [], [], []
    for w in stacked:
        _, r, c = w.shape
        rpb = next(k for k in range(2 * SUBLANES, r + 1, 2 * SUBLANES)
                   if r % k == 0 and r // k <= steps)
        last = r // rpb - 1
        in_specs.append(pl.BlockSpec((None, rpb, c), lambda i, last=last: (layer, jnp.minimum(i, last), 0)))
        out_specs.append(pl.BlockSpec((rpb, c), lambda i, last=last: (jnp.minimum(i, last), 0)))
        out_shapes.append(jax.ShapeDtypeStruct((r, c), BF16))
    return in_specs, out_specs, out_shapes


def _cast_riders(in_refs, out_refs):
    for src, dst in zip(in_refs, out_refs):
        dst[...] = src[...].astype(dst.dtype)


def _ffn1_win_kernel(n_riders, x_ref, x_first_ref, x_next_ref, g1_ref, wg_ref, wu_ref, wd_ref, gm_ref,
                     win_ref, gq_ref, gk_ref, *refs):
    x1_ref, ucp_ref, qT_ref, ka_ref, vT_ref = refs[n_riders:n_riders + 5]
    h_ref = refs[-1]
    _cast_riders(refs[:n_riders], refs[n_riders + 5:-1])
    step = pl.program_id(0)
    halves = _row_halves(x_ref.shape[0])

    @pl.when(step == 0)
    def _():
        h_ref[0] = _rms(x_first_ref[...], g1_ref[...]).astype(BF16)

    xs = [x_ref[r, :] for r in halves]
    ys = _swiglu_halves([h_ref[step % 2, r, :] for r in halves], wg_ref, wu_ref, wd_ref)
    h_ref[1 - step % 2] = _rms(x_next_ref[...], g1_ref[...]).astype(BF16)
    x1s = [x + 0.5 * y for x, y in zip(xs, ys)]
    hms = [_rms(x1, gm_ref[...]).astype(BF16) for x1 in x1s]
    c0 = 2 * D_CONV + D_POOL
    u_attn = [jnp.dot(hm, win_ref[:, c0:], preferred_element_type=F32) for hm in hms]
    for r, u in zip(halves, u_attn):
        _attention_operands(u[:, :D_ATTN], u[:, D_ATTN:2 * D_ATTN], u[:, 2 * D_ATTN:], r,
                            gq_ref, gk_ref, qT_ref, ka_ref, vT_ref)
    for r, x1, hm in zip(halves, x1s, hms):
        x1_ref[r, :] = x1
        ucp_ref[r, :] = jnp.dot(hm, win_ref[:, :c0], preferred_element_type=F32)


def _ffn1_win(x, g1, wg, wu, wd, gm, win, gq, gk, seq, riders, layer):
    rows, d = x.shape
    f = wg.shape[1]
    tm = _row_block(rows)
    c0 = 2 * D_CONV + D_POOL
    nblk = seq // tm
    assert tm == _attn_block(seq) and rows % seq == 0
    row_spec = lambda w: pl.BlockSpec((tm, w), lambda i: (i, 0))
    r_in, r_out, r_shapes = _rider_specs(riders, layer, rows // tm)
    outs = pl.pallas_call(
        functools.partial(_ffn1_win_kernel, len(riders)),
        grid=(rows // tm,),
        in_specs=[row_spec(d), pl.BlockSpec((tm, d), lambda i: (0, 0)),
                  pl.BlockSpec((tm, d), lambda i: (jnp.minimum(i + 1, rows // tm - 1), 0)),
                  _resident((1, d)), _resident((d, f)), _resident((d, f)),
                  _resident((f, d)), _resident((1, d)), _resident(win.shape),
                  _resident(gq.shape), _resident(gk.shape)] + r_in,
        out_specs=[row_spec(d), row_spec(c0),
                   pl.BlockSpec((None, N_HEADS, 2, V_DIM, tm), lambda i: (i // nblk, 0, 0, 0, i % nblk)),
                   pl.BlockSpec((None, N_HEADS, 2, tm, V_DIM), lambda i: (i // nblk, 0, 0, i % nblk, 0)),
                   pl.BlockSpec((None, N_HEADS, None, V_ROWS, tm),
                                lambda i: (i // nblk, 0, i % nblk, 0, 0))] + r_out,
        out_shape=[jax.ShapeDtypeStruct((rows, d), F32),
                   jax.ShapeDtypeStruct((rows, c0), F32),
                   jax.ShapeDtypeStruct((rows // seq, N_HEADS, 2, V_DIM, seq), BF16),
                   jax.ShapeDtypeStruct((rows // seq, N_HEADS, 2, seq, V_DIM), BF16),
                   jax.ShapeDtypeStruct((rows // seq, N_HEADS, nblk, V_ROWS, tm), BF16)] + r_shapes,
        scratch_shapes=[pltpu.VMEM((2, tm, d), BF16)],
        compiler_params=pltpu.CompilerParams(dimension_semantics=("arbitrary",),
                                             vmem_limit_bytes=VMEM_LIMIT_BYTES),
        name="ffn1_win",
    )(x, x, x, g1, wg, wu, wd, gm, win, gq, gk, *riders)
    return (*outs[:5], list(outs[5:]))


def _convpool_tile(t0, seq, main_ref, prev_ref, next_ref, params, scratch, out_ref):
    dw_ref, dwb_ref, lng_ref, lnb_ref, pw_ref, ps_ref = params
    zext, zrot, pext, a1, a2, a3 = scratch
    ts = main_ref.shape[0]
    has_prev = t0 > 0
    has_next = t0 + ts < seq

    def glu(u):
        return u[:, :D_CONV] * _sigmoid(u[:, D_CONV:2 * D_CONV])

    def assemble():
        um = main_ref[...]
        up = prev_ref[...]
        un = next_ref[...]
        zero_halo = jnp.zeros((HALO, D_CONV), F32)
        zext[0:HALO, :] = jnp.where(has_prev, glu(up), zero_halo)
        zext[HALO:HALO + ts, :] = glu(um)
        zext[HALO + ts:, :] = jnp.where(has_next, glu(un), zero_halo)
        pext[0:HALO, :] = jnp.where(has_prev, up[:, 2 * D_CONV:], zero_halo)
        pext[HALO:HALO + ts, :] = um[:, 2 * D_CONV:]
        pext[HALO + ts:, :] = jnp.where(has_next, un[:, 2 * D_CONV:], zero_halo)

    def conv():
        off0 = HALO - CONV_WIDTH // 2
        nz = ts + 2 * HALO - SUBLANES
        for r in range(1, SUBLANES):
            zrot[r - 1, 0:nz, :] = zext[r:r + nz, :]
        acc = jnp.zeros((ts, D_CONV), F32)
        for j in range(CONV_WIDTH):
            r, a = (off0 + j) % SUBLANES, (off0 + j) // SUBLANES * SUBLANES
            tap = zext[a:a + ts, :] if r == 0 else zrot[r - 1, a:a + ts, :]
            acc = acc + dw_ref[j:j + 1, :] * tap
        z = acc + dwb_ref[...]
        mu = jnp.mean(z, axis=-1, keepdims=True)
        zc = z - mu
        var = jnp.mean(zc * zc, axis=-1, keepdims=True)
        y = zc * lax.rsqrt(var + NORM_EPS) * lng_ref[...] + lnb_ref[...]
        out_ref[:, :D_CONV] = (y * _sigmoid(y)).astype(out_ref.dtype)

    def pool():
        n = ts + 2 * HALO
        a1[1:n, :] = pext[0:n - 1, :] + pext[1:n, :]
        a2[2:n - 1, :] = a1[1:n - 2, :] + a1[3:n, :]
        a3[4:n - 3, :] = a2[2:n - 5, :] + a2[6:n - 1, :]
        w2 = a1[HALO:HALO + ts, :]
        w4 = a2[HALO:HALO + ts, :]
        w8 = a3[HALO:HALO + ts, :]
        w16 = a3[HALO - 4:HALO - 4 + ts, :] + a3[HALO + 4:HALO + 4 + ts, :]
        grp = lax.broadcasted_iota(jnp.int32, (ts, D_POOL), 1) // POOL_GROUP_DIM
        win = jnp.where(grp == 0, w2, jnp.where(grp == 1, w4, jnp.where(grp == 2, w8, w16)))
        t = lax.broadcasted_iota(jnp.int32, (ts, D_POOL), 0) + t0
        half = jnp.left_shift(1, grp)
        cnt = jnp.minimum(t + half, seq) - jnp.maximum(t - half, 0)
        d = win / cnt.astype(F32) - pext[HALO:HALO + ts, :]
        yp = jnp.dot(d.astype(BF16), pw_ref[...], preferred_element_type=F32) * ps_ref[...]
        out_ref[:, D_CONV:] = yp.astype(out_ref.dtype)

    return [assemble, conv, pool]


def _convpool_kernel(seq, main_ref, prev_ref, next_ref, dw_ref, dwb_ref, lng_ref, lnb_ref,
                     pw_ref, ps_ref, out_ref, *scratch):
    t0 = pl.program_id(1) * main_ref.shape[0]
    for stage in _convpool_tile(t0, seq, main_ref, prev_ref, next_ref,
                                (dw_ref, dwb_ref, lng_ref, lnb_ref, pw_ref, ps_ref), scratch, out_ref):
        stage()


def _convpool(ucp, dw, dwb, lng, lnb, pw_bd, ps):
    b, seq, c0 = ucp.shape
    ts = _attn_block(seq)
    nh = ts // HALO
    last_halo = seq // HALO - 1
    n = ts + 2 * HALO
    return pl.pallas_call(
        functools.partial(_convpool_kernel, seq),
        grid=(b, seq // ts),
        in_specs=[pl.BlockSpec((None, ts, c0), lambda bi, i: (bi, i, 0)),
                  pl.BlockSpec((None, HALO, c0), lambda bi, i: (bi, jnp.maximum(i * nh - 1, 0), 0)),
                  pl.BlockSpec((None, HALO, c0),
                               lambda bi, i: (bi, jnp.minimum((i + 1) * nh, last_halo), 0)),
                  _resident(dw.shape), _resident(dwb.shape), _resident(lng.shape),
                  _resident(lnb.shape), _resident(pw_bd.shape), _resident(ps.shape)],
        out_specs=pl.BlockSpec((None, ts, D_CONV + D_POOL), lambda bi, i: (bi, i, 0)),
        out_shape=jax.ShapeDtypeStruct((b, seq, D_CONV + D_POOL), BF16),
        scratch_shapes=[pltpu.VMEM((n, D_CONV), F32), pltpu.VMEM((SUBLANES - 1, n, D_CONV), F32),
                        pltpu.VMEM((n, D_POOL), F32), pltpu.VMEM((n, D_POOL), F32),
                        pltpu.VMEM((n, D_POOL), F32), pltpu.VMEM((n, D_POOL), F32)],
        compiler_params=pltpu.CompilerParams(dimension_semantics=("arbitrary", "arbitrary"),
                                             vmem_limit_bytes=VMEM_LIMIT_BYTES),
        name="convpool",
    )(ucp, ucp, ucp, dw, dwb, lng, lnb, pw_bd, ps)


def _attn_kernel(lambda_init, unshifted, n_blocks, slopes_ref, qT_ref, ka_ref, vT_ref, lq1_ref, lk1_ref,
                 lq2_ref, lk2_ref, sub_ref, out_ref, qv_ref, m_ref, acc2_ref):
    f = pl.program_id(0)
    nk = ka_ref.shape[1]
    tk = ka_ref.shape[2]
    tq = qT_ref.shape[2]
    block = jnp.minimum(f, n_blocks - 1)
    i = block % nk
    slope = slopes_ref[(block // nk) % N_HEADS]
    acc_ref = acc2_ref.at[f % 2]

    @pl.when(f == 0)
    def _():
        acc2_ref[1] = jnp.ones(acc2_ref.shape[1:], F32)

    lam = (jnp.exp(jnp.sum(lq1_ref[...] * lk1_ref[...], axis=-1, keepdims=True))
           - jnp.exp(jnp.sum(lq2_ref[...] * lk2_ref[...], axis=-1, keepdims=True))
           + lambda_init)
    a0 = acc2_ref[1 - f % 2, 0]
    a1 = acc2_ref[1 - f % 2, 1]
    o = a0[:V_DIM] / a0[V_DIM:V_DIM + 1] - lam * (a1[:V_DIM] / a1[V_DIM:V_DIM + 1])
    ms = jnp.mean(o * o, axis=0, keepdims=True)
    y = o * lax.rsqrt(ms + NORM_EPS) * sub_ref[...] * (1.0 - lambda_init)
    out_ref[...] = y.T.astype(out_ref.dtype)

    row = lax.broadcasted_iota(jnp.int32, (V_DIM, tq), 0)
    for c in range(2):
        q = qT_ref[c]
        qv_ref[0, c] = jnp.where(row < HEAD_DIM, q, -q)
        qv_ref[1, c] = jnp.where(row < HEAD_DIM, q, jnp.zeros_like(q))
        qv_ref[2, c] = q
    acc_ref[...] = jnp.zeros(acc_ref.shape, F32)

    def diag_unit():
        kk = lax.broadcasted_iota(jnp.int32, (tk, tq), 0)
        qq = lax.broadcasted_iota(jnp.int32, (tk, tq), 1)
        return i, 1, slope * jnp.abs(kk - qq).astype(F32)

    def off_diag_unit(t):
        j = jnp.where(t >= i, t + 1, t)
        dist = jnp.abs(j - i) * tk
        return j, jnp.where(j > i, 2, 0), slope * jnp.full((1, tq), dist, jnp.int32).astype(F32)

    def scores(unit, c):
        j, side, bias = unit
        return jnp.dot(ka_ref[c, j], qv_ref[side, c], preferred_element_type=F32) - bias

    def unshifted_chunk(units):
        chains = [(u, c) for u in units for c in range(2)]
        probs = [None] * len(chains)
        pv = [[], []]

        def score_stage(k):
            probs[k] = jnp.exp(scores(*chains[k])).astype(BF16)

        def pv_stage(k):
            (j, _, _), c = chains[k]
            pv[c].append(jnp.dot(vT_ref[j], probs[k], preferred_element_type=F32))

        score_stage(0)
        for k in range(1, len(chains)):
            score_stage(k)
            pv_stage(k - 1)
        pv_stage(len(chains) - 1)
        for c in range(2):
            acc_ref[c] += functools.reduce(lambda a, b: a + b, pv[c])

    def running_max_step(unit):
        for c in range(2):
            s = scores(unit, c)
            m_old = m_ref[c]
            m_new = jnp.maximum(m_old, jnp.max(s, axis=0, keepdims=True))
            p = jnp.exp(s - m_new).astype(BF16)
            alpha = jnp.exp(m_old - m_new)
            acc_ref[c] = acc_ref[c] * alpha + jnp.dot(vT_ref[unit[0]], p, preferred_element_type=F32)
            m_ref[c] = m_new

    if unshifted:
        chunk = min(UNSHIFTED_CHUNK, nk)
        unshifted_chunk([diag_unit()] + [off_diag_unit(t) for t in range(chunk - 1)])

        def body(g, carry):
            t0 = chunk - 1 + g * chunk
            unshifted_chunk([off_diag_unit(t0 + r) for r in range(chunk)])
            return carry

        lax.fori_loop(0, nk // chunk - 1, body, 0)
    else:
        m_ref[...] = jnp.full(m_ref.shape, NEG_BIG, F32)
        running_max_step(diag_unit())

        def body(t, carry):
            running_max_step(off_diag_unit(t))
            return carry

        lax.fori_loop(0, nk - 1, body, 0)


def _attention(qT, ka, vT, lq1, lk1, lq2, lk2, sub_col, lambda_init, unshifted):
    b, nh, _, kw, seq = qT.shape
    blk = vT.shape[-1]
    nblk = seq // blk
    assert kw == V_DIM and nh == N_HEADS and nblk % min(UNSHIFTED_CHUNK, nblk) == 0
    ka = ka.reshape(b, nh, 2, nblk, blk, kw)
    slopes = jnp.asarray([_slope(h) for h in range(nh)], F32)
    n_blocks = b * nh * nblk

    def where(f):
        return f // (nh * nblk), (f // nblk) % nh, f % nblk

    def computed(f):
        return where(jnp.minimum(f, n_blocks - 1))

    def written(f):
        return where(jnp.maximum(f - 1, 0))

    return pl.pallas_call(
        functools.partial(_attn_kernel, lambda_init, unshifted, n_blocks),
        grid=(n_blocks + 1,),
        in_specs=[pl.BlockSpec(memory_space=pltpu.SMEM),
                  pl.BlockSpec((None, None, 2, kw, blk),
                               lambda f: (computed(f)[0], computed(f)[1], 0, 0, computed(f)[2])),
                  pl.BlockSpec((None, None, 2, nblk, blk, kw),
                               lambda f: (computed(f)[0], computed(f)[1], 0, 0, 0, 0)),
                  pl.BlockSpec((None, None, nblk, V_ROWS, blk),
                               lambda f: (computed(f)[0], computed(f)[1], 0, 0, 0)),
                  _resident(lq1.shape), _resident(lk1.shape), _resident(lq2.shape),
                  _resident(lk2.shape), _resident(sub_col.shape)],
        out_specs=pl.BlockSpec((None, blk, V_DIM),
                               lambda f: (written(f)[0], written(f)[2], written(f)[1])),
        out_shape=jax.ShapeDtypeStruct((b, seq, nh * V_DIM), BF16),
        scratch_shapes=[pltpu.VMEM((3, 2, kw, blk), qT.dtype),
                        pltpu.VMEM((2, 1, blk), F32),
                        pltpu.VMEM((2, 2, V_ROWS, blk), F32)],
        compiler_params=pltpu.CompilerParams(
            dimension_semantics=("arbitrary",),
            vmem_limit_bytes=VMEM_LIMIT_BYTES),
        name="attention_unshifted" if unshifted else "attention_running_max",
    )(slopes, qT, ka, vT, lq1, lk1, lq2, lk2, sub_col)


def _diff_attention(qT, ka, vT, q_gain, k_gain, lq1, lk1, lq2, lk2, sub_col, lambda_init):
    score_bound = (1.01 * HEAD_DIM ** 0.5) * jnp.max(jnp.abs(q_gain)) * jnp.max(jnp.abs(k_gain))

    def branch(unshifted):
        return lambda qT, ka, vT: _attention(qT, ka, vT, lq1, lk1, lq2, lk2, sub_col, lambda_init,
                                             unshifted)

    return lax.cond(score_bound <= MAX_UNSHIFTED_SCORE, branch(True), branch(False), qT, ka, vT)


def _out_ffn2_kernel(n_riders, x1_ref, ycp_ref, yat_ref, wo_ref, g2_ref, wg_ref, wu_ref,
                     wd_ref, gp_ref, *refs):
    d_cp = ycp_ref.shape[1]
    out_ref = refs[n_riders]
    _cast_riders(refs[:n_riders], refs[n_riders + 1:])
    halves = _row_halves(x1_ref.shape[0])
    x2s = [x1_ref[r, :]
           + jnp.dot(ycp_ref[r, :], wo_ref[:d_cp, :], preferred_element_type=F32)
           + jnp.dot(yat_ref[r, :], wo_ref[d_cp:, :], preferred_element_type=F32) for r in halves]
    ys = _swiglu_halves([_rms(x2, g2_ref[...]) for x2 in x2s], wg_ref, wu_ref, wd_ref)
    for r, x2, y in zip(halves, x2s, ys):
        out_ref[r, :] = _rms(x2 + 0.5 * y, gp_ref[...])


def _out_ffn2(x1, ycp, yat, wo, g2, wg, wu, wd, gp, riders, layer):
    rows, d = x1.shape
    f = wg.shape[1]
    tm = _row_block(rows)
    row_spec = lambda w: pl.BlockSpec((tm, w), lambda i: (i, 0))
    r_in, r_out, r_shapes = _rider_specs(riders, layer, rows // tm)
    outs = pl.pallas_call(
        functools.partial(_out_ffn2_kernel, len(riders)),
        grid=(rows // tm,),
        in_specs=[row_spec(d), row_spec(ycp.shape[1]), row_spec(yat.shape[1]),
                  _resident(wo.shape), _resident((1, d)),
                  _resident((d, f)), _resident((d, f)), _resident((f, d)), _resident((1, d))] + r_in,
        out_specs=[row_spec(d)] + r_out,
        out_shape=[jax.ShapeDtypeStruct((rows, d), F32)] + r_shapes,
        compiler_params=pltpu.CompilerParams(dimension_semantics=("arbitrary",),
                                             vmem_limit_bytes=VMEM_LIMIT_BYTES),
        name="out_ffn2",
    )(x1, ycp, yat, wo, g2, wg, wu, wd, gp, *riders)
    return outs[0], list(outs[1:])


def _pool_block_diag(pool_w):
    g, cg, _ = pool_w.shape
    out = jnp.zeros((g * cg, g * cg), pool_w.dtype)
    for k in range(g):
        out = out.at[k * cg:(k + 1) * cg, k * cg:(k + 1) * cg].set(pool_w[k])
    return out


def kernel(x, ffn1_norm, ffn1_w_gate, ffn1_w_up, ffn1_w_down, mix_norm, w_in, conv_dw, conv_dw_bias, conv_ln_gain, conv_ln_bias, pool_w, pool_scale, q_norm, k_norm, lambda_q1, lambda_k1, lambda_q2, lambda_k2, attn_subln, w_out, ffn2_norm, ffn2_w_gate, ffn2_w_up, ffn2_w_down, post_norm):
    b, seq, d = x.shape
    depth = w_in.shape[0]
    rows = b * seq
    d_cp = D_CONV + D_POOL
    row = lambda v: v.reshape(1, -1).astype(F32)
    xf = x.reshape(rows, d).astype(F32)
    stage1 = [w[0].astype(BF16) for w in (ffn1_w_gate, ffn1_w_up, ffn1_w_down, w_in)]
    for l in range(depth):
        lambda_init = 0.8 - 0.6 * math.exp(-0.3 * l)
        pair = lambda g: jnp.concatenate([g, g]).reshape(1, -1).astype(F32)
        x1, ucp, qT, ka, vT, stage3 = _ffn1_win(
            xf, row(ffn1_norm[l]), stage1[0], stage1[1], stage1[2], row(mix_norm[l]), stage1[3],
            pair(q_norm[l]), pair(k_norm[l]), seq,
            [ffn2_w_gate, ffn2_w_up, ffn2_w_down, w_out], l)
        ycp = _convpool(
            ucp.reshape(b, seq, -1), conv_dw[l].astype(F32), row(conv_dw_bias[l]),
            row(conv_ln_gain[l]), row(conv_ln_bias[l]),
            _pool_block_diag(pool_w[l]).astype(BF16), row(pool_scale[l]))
        yat = _diff_attention(
            qT, ka, vT, q_norm[l], k_norm[l], row(lambda_q1[l]), row(lambda_k1[l]), row(lambda_q2[l]),
            row(lambda_k2[l]), attn_subln[l].reshape(-1, 1).astype(F32), lambda_init)
        nxt = l + 1
        xf, stage1 = _out_ffn2(
            x1, ycp.reshape(rows, d_cp), yat.reshape(rows, -1), stage3[3],
            row(ffn2_norm[l]), stage3[0], stage3[1], stage3[2], row(post_norm[l]),
            [ffn1_w_gate, ffn1_w_up, ffn1_w_down, w_in] if nxt < depth else [], nxt)
    return xf.reshape(b, seq, d).astype(x.dtype)
```

```python
import functools
import math

import jax
import jax.numpy as jnp
from jax import lax
from jax.experimental import pallas as pl
from jax.experimental.pallas import tpu as pltpu

F32 = jnp.float32
BF16 = jnp.bfloat16

NORM_EPS = 1e-6
N_HEADS = 4
HEAD_DIM = 64
V_DIM = 2 * HEAD_DIM
D_CONV = 256
D_POOL = 256
D_ATTN = 512
CONV_WIDTH = 31
POOL_GROUP_DIM = 64
SUBLANES = 8
HALO = 16
V_ROWS = V_DIM + 8
BF16_EXACT_INT = 256
VMEM_LIMIT_BYTES = 56 * 1024 * 1024
NEG_BIG = -1e30
MAX_UNSHIFTED_SCORE = 60.0
UNSHIFTED_CHUNK = 16


def _attn_block(seq):
    return min(512, seq)


def _row_block(rows):
    return min(512, rows)


def _rms(x, g):
    ms = jnp.mean(x * x, axis=-1, keepdims=True)
    return x * lax.rsqrt(ms + NORM_EPS) * g


def _sigmoid(x):
    return 1.0 / (1.0 + jnp.exp(-x))


def _row_halves(tm):
    half = tm // 2 if tm % (2 * SUBLANES) == 0 else tm
    return [slice(r, r + half) for r in range(0, tm, half)]


def _swiglu_halves(hs, wg_ref, wu_ref, wd_ref):
    hbs = [h.astype(BF16) for h in hs]
    gus = [(jnp.dot(hb, wg_ref[...], preferred_element_type=F32),
            jnp.dot(hb, wu_ref[...], preferred_element_type=F32)) for hb in hbs]
    acts = [(g * _sigmoid(g) * u).astype(BF16) for g, u in gus]
    return [jnp.dot(a, wd_ref[...], preferred_element_type=F32) for a in acts]


def _resident(shape):
    nd = len(shape)
    return pl.BlockSpec(shape, lambda *_: (0,) * nd, pipeline_mode=pl.Buffered(1))


def _slope(h):
    return 2.0 ** (-8.0 * (h + 1) / N_HEADS)


def _attention_operands(uq, uk, uv, rows, gq_ref, gk_ref, qT_ref, ka_ref, vT_ref):
    n = uq.shape[0]
    blk = qT_ref.shape[-1]
    lane = lax.broadcasted_iota(jnp.int32, (n, V_DIM), 1)
    low = lane < HEAD_DIM
    pos = lax.broadcasted_iota(jnp.int32, (n, V_DIM), 0) + rows.start
    one = jnp.ones((n, V_DIM), F32)
    zero = jnp.zeros((n, V_DIM), F32)
    row8 = lax.broadcasted_iota(jnp.int32, (V_ROWS - V_DIM, n), 0)
    ones_rows = jnp.where(row8 == 0, 1.0, 0.0).astype(vT_ref.dtype)
    digits = [(pos % BF16_EXACT_INT).astype(F32)]
    place = BF16_EXACT_INT
    while place < blk:
        digits.append((pos % (place * BF16_EXACT_INT) - pos % place).astype(F32))
        place *= BF16_EXACT_INT

    def norm_pair(x, g):
        x2 = x * x
        s_lo = jnp.sum(jnp.where(low, x2, 0.0), axis=-1, keepdims=True)
        s_hi = jnp.sum(jnp.where(low, 0.0, x2), axis=-1, keepdims=True)
        ms = jnp.where(low, s_lo, s_hi) * (1.0 / HEAD_DIM)
        return x * lax.rsqrt(ms + NORM_EPS) * g

    def aug(cols):
        a = zero
        for k, col in enumerate(cols):
            a = jnp.where(lane == HEAD_DIM + k, col, a)
        return a

    def other_map(x):
        return pltpu.roll(x, HEAD_DIM, 1)

    for h in range(N_HEADS):
        cols = slice(h * V_DIM, (h + 1) * V_DIM)
        qn = norm_pair(uq[:, cols], gq_ref[...]) * (HEAD_DIM ** -0.5)
        kn = norm_pair(uk[:, cols], gk_ref[...])
        coef = _slope(h)
        q_aug = aug([-coef * one] * len(digits) + digits)
        k_aug = aug(digits + [coef * one] * len(digits))
        for c in range(2):
            qc = qn if c == 0 else other_map(qn)
            kc = kn if c == 0 else other_map(kn)
            qT_ref[h, c, :, rows] = jnp.where(low, qc, q_aug).T.astype(qT_ref.dtype)
            ka_ref[h, c, rows, :] = jnp.where(low, kc, k_aug).astype(ka_ref.dtype)
        vT_ref[h, :V_DIM, rows] = uv[:, cols].T.astype(vT_ref.dtype)
        vT_ref[h, V_DIM:, rows] = ones_rows


def _rider_specs(stacked, layer, steps):
    in_specs, out_specs, out_shapes = [], [], []
    for w in stacked:
        _, r, c = w.shape
        rpb = next(k for k in range(2 * SUBLANES, r + 1, 2 * SUBLANES)
                   if r % k == 0 and r // k <= steps)
        last = r // rpb - 1
        in_specs.append(pl.BlockSpec((None, rpb, c), lambda i, last=last: (layer, jnp.minimum(i, last), 0)))
        out_specs.append(pl.BlockSpec((rpb, c), lambda i, last=last: (jnp.minimum(i, last), 0)))
        out_shapes.append(jax.ShapeDtypeStruct((r, c), BF16))
    return in_specs, out_specs, out_shapes


def _cast_riders(in_refs, out_refs):
    for src, dst in zip(in_refs, out_refs):
        dst[...] = src[...].astype(dst.dtype)


def _ffn1_win_kernel(n_riders, x_ref, g1_ref, wg_ref, wu_ref, wd_ref, gm_ref, win_ref, gq_ref, gk_ref,
                     *refs):
    x1_ref, ucp_ref, qT_ref, ka_ref, vT_ref = refs[n_riders:n_riders + 5]
    _cast_riders(refs[:n_riders], refs[n_riders + 5:])
    halves = _row_halves(x_ref.shape[0])
    xs = [x_ref[r, :] for r in halves]
    ys = _swiglu_halves([_rms(x, g1_ref[...]) for x in xs], wg_ref, wu_ref, wd_ref)
    x1s = [x + 0.5 * y for x, y in zip(xs, ys)]
    hms = [_rms(x1, gm_ref[...]).astype(BF16) for x1 in x1s]
    c0 = 2 * D_CONV + D_POOL
    u_attn = [jnp.dot(hm, win_ref[:, c0:], preferred_element_type=F32) for hm in hms]
    for r, u in zip(halves, u_attn):
        _attention_operands(u[:, :D_ATTN], u[:, D_ATTN:2 * D_ATTN], u[:, 2 * D_ATTN:], r,
                            gq_ref, gk_ref, qT_ref, ka_ref, vT_ref)
    for r, x1, hm in zip(halves, x1s, hms):
        x1_ref[r, :] = x1
        ucp_ref[r, :] = jnp.dot(hm, win_ref[:, :c0], preferred_element_type=F32)


def _ffn1_win(x, g1, wg, wu, wd, gm, win, gq, gk, seq, riders, layer):
    rows, d = x.shape
    f = wg.shape[1]
    tm = _row_block(rows)
    c0 = 2 * D_CONV + D_POOL
    nblk = seq // tm
    assert tm == _attn_block(seq) and rows % seq == 0
    row_spec = lambda w: pl.BlockSpec((tm, w), lambda i: (i, 0))
    r_in, r_out, r_shapes = _rider_specs(riders, layer, rows // tm)
    outs = pl.pallas_call(
        functools.partial(_ffn1_win_kernel, len(riders)),
        grid=(rows // tm,),
        in_specs=[row_spec(d), _resident((1, d)), _resident((d, f)), _resident((d, f)),
                  _resident((f, d)), _resident((1, d)), _resident(win.shape),
                  _resident(gq.shape), _resident(gk.shape)] + r_in,
        out_specs=[row_spec(d), row_spec(c0),
                   pl.BlockSpec((None, N_HEADS, 2, V_DIM, tm), lambda i: (i // nblk, 0, 0, 0, i % nblk)),
                   pl.BlockSpec((None, N_HEADS, 2, tm, V_DIM), lambda i: (i // nblk, 0, 0, i % nblk, 0)),
                   pl.BlockSpec((None, N_HEADS, None, V_ROWS, tm),
                                lambda i: (i // nblk, 0, i % nblk, 0, 0))] + r_out,
        out_shape=[jax.ShapeDtypeStruct((rows, d), F32),
                   jax.ShapeDtypeStruct((rows, c0), F32),
                   jax.ShapeDtypeStruct((rows // seq, N_HEADS, 2, V_DIM, seq), BF16),
                   jax.ShapeDtypeStruct((rows // seq, N_HEADS, 2, seq, V_DIM), BF16),
                   jax.ShapeDtypeStruct((rows // seq, N_HEADS, nblk, V_ROWS, tm), BF16)] + r_shapes,
        compiler_params=pltpu.CompilerParams(dimension_semantics=("arbitrary",),
                                             vmem_limit_bytes=VMEM_LIMIT_BYTES),
        name="ffn1_win",
    )(x, g1, wg, wu, wd, gm, win, gq, gk, *riders)
    return (*outs[:5], list(outs[5:]))


def _convpool_tile(t0, seq, main_ref, prev_ref, next_ref, params, scratch, out_ref):
    dw_ref, dwb_ref, lng_ref, lnb_ref, pw_ref, ps_ref = params
    zext, zrot, pext, a1, a2, a3 = scratch
    ts = main_ref.shape[0]
    has_prev = t0 > 0
    has_next = t0 + ts < seq

    def glu(u):
        return u[:, :D_CONV] * _sigmoid(u[:, D_CONV:2 * D_CONV])

    def assemble():
        um = main_ref[...]
        up = prev_ref[...]
        un = next_ref[...]
        zero_halo = jnp.zeros((HALO, D_CONV), F32)
        zext[0:HALO, :] = jnp.where(has_prev, glu(up), zero_halo)
        zext[HALO:HALO + ts, :] = glu(um)
        zext[HALO + ts:, :] = jnp.where(has_next, glu(un), zero_halo)
        pext[0:HALO, :] = jnp.where(has_prev, up[:, 2 * D_CONV:], zero_halo)
        pext[HALO:HALO + ts, :] = um[:, 2 * D_CONV:]
        pext[HALO + ts:, :] = jnp.where(has_next, un[:, 2 * D_CONV:], zero_halo)

    def conv():
        off0 = HALO - CONV_WIDTH // 2
        nz = ts + 2 * HALO - SUBLANES
        for r in range(1, SUBLANES):
            zrot[r - 1, 0:nz, :] = zext[r:r + nz, :]
        acc = jnp.zeros((ts, D_CONV), F32)
        for j in range(CONV_WIDTH):
            r, a = (off0 + j) % SUBLANES, (off0 + j) // SUBLANES * SUBLANES
            tap = zext[a:a + ts, :] if r == 0 else zrot[r - 1, a:a + ts, :]
            acc = acc + dw_ref[j:j + 1, :] * tap
        z = acc + dwb_ref[...]
        mu = jnp.mean(z, axis=-1, keepdims=True)
        zc = z - mu
        var = jnp.mean(zc * zc, axis=-1, keepdims=True)
        y = zc * lax.rsqrt(var + NORM_EPS) * lng_ref[...] + lnb_ref[...]
        out_ref[:, :D_CONV] = (y * _sigmoid(y)).astype(out_ref.dtype)

    def pool():
        n = ts + 2 * HALO
        a1[1:n, :] = pext[0:n - 1, :] + pext[1:n, :]
        a2[2:n - 1, :] = a1[1:n - 2, :] + a1[3:n, :]
        a3[4:n - 3, :] = a2[2:n - 5, :] + a2[6:n - 1, :]
        w2 = a1[HALO:HALO + ts, :]
        w4 = a2[HALO:HALO + ts, :]
        w8 = a3[HALO:HALO + ts, :]
        w16 = a3[HALO - 4:HALO - 4 + ts, :] + a3[HALO + 4:HALO + 4 + ts, :]
        grp = lax.broadcasted_iota(jnp.int32, (ts, D_POOL), 1) // POOL_GROUP_DIM
        win = jnp.where(grp == 0, w2, jnp.where(grp == 1, w4, jnp.where(grp == 2, w8, w16)))
        t = lax.broadcasted_iota(jnp.int32, (ts, D_POOL), 0) + t0
        half = jnp.left_shift(1, grp)
        cnt = jnp.minimum(t + half, seq) - jnp.maximum(t - half, 0)
        d = win / cnt.astype(F32) - pext[HALO:HALO + ts, :]
        yp = jnp.dot(d.astype(BF16), pw_ref[...], preferred_element_type=F32) * ps_ref[...]
        out_ref[:, D_CONV:] = yp.astype(out_ref.dtype)

    return [assemble, conv, pool]


def _convpool_kernel(seq, main_ref, prev_ref, next_ref, dw_ref, dwb_ref, lng_ref, lnb_ref,
                     pw_ref, ps_ref, out_ref, *scratch):
    t0 = pl.program_id(1) * main_ref.shape[0]
    for stage in _convpool_tile(t0, seq, main_ref, prev_ref, next_ref,
                                (dw_ref, dwb_ref, lng_ref, lnb_ref, pw_ref, ps_ref), scratch, out_ref):
        stage()


def _convpool(ucp, dw, dwb, lng, lnb, pw_bd, ps):
    b, seq, c0 = ucp.shape
    ts = min(2 * _attn_block(seq), seq)
    nh = ts // HALO
    last_halo = seq // HALO - 1
    n = ts + 2 * HALO
    return pl.pallas_call(
        functools.partial(_convpool_kernel, seq),
        grid=(b, seq // ts),
        in_specs=[pl.BlockSpec((None, ts, c0), lambda bi, i: (bi, i, 0)),
                  pl.BlockSpec((None, HALO, c0), lambda bi, i: (bi, jnp.maximum(i * nh - 1, 0), 0)),
                  pl.BlockSpec((None, HALO, c0),
                               lambda bi, i: (bi, jnp.minimum((i + 1) * nh, last_halo), 0)),
                  _resident(dw.shape), _resident(dwb.shape), _resident(lng.shape),
                  _resident(lnb.shape), _resident(pw_bd.shape), _resident(ps.shape)],
        out_specs=pl.BlockSpec((None, ts, D_CONV + D_POOL), lambda bi, i: (bi, i, 0)),
        out_shape=jax.ShapeDtypeStruct((b, seq, D_CONV + D_POOL), BF16),
        scratch_shapes=[pltpu.VMEM((n, D_CONV), F32), pltpu.VMEM((SUBLANES - 1, n, D_CONV), F32),
                        pltpu.VMEM((n, D_POOL), F32), pltpu.VMEM((n, D_POOL), F32),
                        pltpu.VMEM((n, D_POOL), F32), pltpu.VMEM((n, D_POOL), F32)],
        compiler_params=pltpu.CompilerParams(dimension_semantics=("arbitrary", "arbitrary"),
                                             vmem_limit_bytes=VMEM_LIMIT_BYTES),
        name="convpool",
    )(ucp, ucp, ucp, dw, dwb, lng, lnb, pw_bd, ps)


def _attn_kernel(lambda_init, unshifted, n_blocks, slopes_ref, qT_ref, ka_ref, vT_ref, lq1_ref, lk1_ref,
                 lq2_ref, lk2_ref, sub_ref, out_ref, qv_ref, m_ref, acc2_ref):
    f = pl.program_id(0)
    nk = ka_ref.shape[1]
    tk = ka_ref.shape[2]
    tq = qT_ref.shape[2]
    block = jnp.minimum(f, n_blocks - 1)
    i = block % nk
    slope = slopes_ref[(block // nk) % N_HEADS]
    acc_ref = acc2_ref.at[f % 2]

    @pl.when(f == 0)
    def _():
        acc2_ref[1] = jnp.ones(acc2_ref.shape[1:], F32)

    lam = (jnp.exp(jnp.sum(lq1_ref[...] * lk1_ref[...], axis=-1, keepdims=True))
           - jnp.exp(jnp.sum(lq2_ref[...] * lk2_ref[...], axis=-1, keepdims=True))
           + lambda_init)
    a0 = acc2_ref[1 - f % 2, 0]
    a1 = acc2_ref[1 - f % 2, 1]
    o = a0[:V_DIM] / a0[V_DIM:V_DIM + 1] - lam * (a1[:V_DIM] / a1[V_DIM:V_DIM + 1])
    ms = jnp.mean(o * o, axis=0, keepdims=True)
    y = o * lax.rsqrt(ms + NORM_EPS) * sub_ref[...] * (1.0 - lambda_init)
    out_ref[...] = y.T.astype(out_ref.dtype)

    row = lax.broadcasted_iota(jnp.int32, (V_DIM, tq), 0)
    for c in range(2):
        q = qT_ref[c]
        qv_ref[0, c] = jnp.where(row < HEAD_DIM, q, -q)
        qv_ref[1, c] = jnp.where(row < HEAD_DIM, q, jnp.zeros_like(q))
        qv_ref[2, c] = q
    acc_ref[...] = jnp.zeros(acc_ref.shape, F32)

    def diag_unit():
        kk = lax.broadcasted_iota(jnp.int32, (tk, tq), 0)
        qq = lax.broadcasted_iota(jnp.int32, (tk, tq), 1)
        return i, 1, slope * jnp.abs(kk - qq).astype(F32)

    def off_diag_unit(t):
        j = jnp.where(t >= i, t + 1, t)
        dist = jnp.abs(j - i) * tk
        return j, jnp.where(j > i, 2, 0), slope * jnp.full((1, tq), dist, jnp.int32).astype(F32)

    def scores(unit, c):
        j, side, bias = unit
        return jnp.dot(ka_ref[c, j], qv_ref[side, c], preferred_element_type=F32) - bias

    def unshifted_chunk(units):
        chains = [(u, c) for u in units for c in range(2)]
        probs = [None] * len(chains)
        pv = [[], []]

        def score_stage(k):
            probs[k] = jnp.exp(scores(*chains[k])).astype(BF16)

        def pv_stage(k):
            (j, _, _), c = chains[k]
            pv[c].append(jnp.dot(vT_ref[j], probs[k], preferred_element_type=F32))

        score_stage(0)
        for k in range(1, len(chains)):
            score_stage(k)
            pv_stage(k - 1)
        pv_stage(len(chains) - 1)
        for c in range(2):
            acc_ref[c] += functools.reduce(lambda a, b: a + b, pv[c])

    def running_max_step(unit):
        for c in range(2):
            s = scores(unit, c)
            m_old = m_ref[c]
            m_new = jnp.maximum(m_old, jnp.max(s, axis=0, keepdims=True))
            p = jnp.exp(s - m_new).astype(BF16)
            alpha = jnp.exp(m_old - m_new)
            acc_ref[c] = acc_ref[c] * alpha + jnp.dot(vT_ref[unit[0]], p, preferred_element_type=F32)
            m_ref[c] = m_new

    if unshifted:
        chunk = min(UNSHIFTED_CHUNK, nk)
        unshifted_chunk([diag_unit()] + [off_diag_unit(t) for t in range(chunk - 1)])

        def body(g, carry):
            t0 = chunk - 1 + g * chunk
            unshifted_chunk([off_diag_unit(t0 + r) for r in range(chunk)])
            return carry

        lax.fori_loop(0, nk // chunk - 1, body, 0)
    else:
        m_ref[...] = jnp.full(m_ref.shape, NEG_BIG, F32)
        running_max_step(diag_unit())

        def body(t, carry):
            running_max_step(off_diag_unit(t))
            return carry

        lax.fori_loop(0, nk - 1, body, 0)


def _attention(qT, ka, vT, lq1, lk1, lq2, lk2, sub_col, lambda_init, unshifted):
    b, nh, _, kw, seq = qT.shape
    blk = vT.shape[-1]
    nblk = seq // blk
    assert kw == V_DIM and nh == N_HEADS and nblk % min(UNSHIFTED_CHUNK, nblk) == 0
    ka = ka.reshape(b, nh, 2, nblk, blk, kw)
    slopes = jnp.asarray([_slope(h) for h in range(nh)], F32)
    n_blocks = b * nh * nblk

    def where(f):
        return f // (nh * nblk), (f // nblk) % nh, f % nblk

    def computed(f):
        return where(jnp.minimum(f, n_blocks - 1))

    def written(f):
        return where(jnp.maximum(f - 1, 0))

    return pl.pallas_call(
        functools.partial(_attn_kernel, lambda_init, unshifted, n_blocks),
        grid=(n_blocks + 1,),
        in_specs=[pl.BlockSpec(memory_space=pltpu.SMEM),
                  pl.BlockSpec((None, None, 2, kw, blk),
                               lambda f: (computed(f)[0], computed(f)[1], 0, 0, computed(f)[2])),
                  pl.BlockSpec((None, None, 2, nblk, blk, kw),
                               lambda f: (computed(f)[0], computed(f)[1], 0, 0, 0, 0)),
                  pl.BlockSpec((None, None, nblk, V_ROWS, blk),
                               lambda f: (computed(f)[0], computed(f)[1], 0, 0, 0)),
                  _resident(lq1.shape), _resident(lk1.shape), _resident(lq2.shape),
                  _resident(lk2.shape), _resident(sub_col.shape)],
        out_specs=pl.BlockSpec((None, blk, V_DIM),
                               lambda f: (written(f)[0], written(f)[2], written(f)[1])),
        out_shape=jax.ShapeDtypeStruct((b, seq, nh * V_DIM), BF16),
        scratch_shapes=[pltpu.VMEM((3, 2, kw, blk), qT.dtype),
                        pltpu.VMEM((2, 1, blk), F32),
                        pltpu.VMEM((2, 2, V_ROWS, blk), F32)],
        compiler_params=pltpu.CompilerParams(
            dimension_semantics=("arbitrary",),
            vmem_limit_bytes=VMEM_LIMIT_BYTES),
        name="attention_unshifted" if unshifted else "attention_running_max",
    )(slopes, qT, ka, vT, lq1, lk1, lq2, lk2, sub_col)


def _diff_attention(qT, ka, vT, q_gain, k_gain, lq1, lk1, lq2, lk2, sub_col, lambda_init):
    score_bound = (1.01 * HEAD_DIM ** 0.5) * jnp.max(jnp.abs(q_gain)) * jnp.max(jnp.abs(k_gain))

    def branch(unshifted):
        return lambda qT, ka, vT: _attention(qT, ka, vT, lq1, lk1, lq2, lk2, sub_col, lambda_init,
                                             unshifted)

    return lax.cond(score_bound <= MAX_UNSHIFTED_SCORE, branch(True), branch(False), qT, ka, vT)


def _out_ffn2_kernel(n_riders, x1_ref, ycp_ref, yat_ref, wo_ref, g2_ref, wg_ref, wu_ref,
                     wd_ref, gp_ref, *refs):
    d_cp = ycp_ref.shape[1]
    out_ref = refs[n_riders]
    _cast_riders(refs[:n_riders], refs[n_riders + 1:])
    halves = _row_halves(x1_ref.shape[0])
    x2s = [x1_ref[r, :]
           + jnp.dot(ycp_ref[r, :], wo_ref[:d_cp, :], preferred_element_type=F32)
           + jnp.dot(yat_ref[r, :], wo_ref[d_cp:, :], preferred_element_type=F32) for r in halves]
    ys = _swiglu_halves([_rms(x2, g2_ref[...]) for x2 in x2s], wg_ref, wu_ref, wd_ref)
    for r, x2, y in zip(halves, x2s, ys):
        out_ref[r, :] = _rms(x2 + 0.5 * y, gp_ref[...])


def _out_ffn2(x1, ycp, yat, wo, g2, wg, wu, wd, gp, riders, layer):
    rows, d = x1.shape
    f = wg.shape[1]
    tm = _row_block(rows)
    row_spec = lambda w: pl.BlockSpec((tm, w), lambda i: (i, 0))
    r_in, r_out, r_shapes = _rider_specs(riders, layer, rows // tm)
    outs = pl.pallas_call(
        functools.partial(_out_ffn2_kernel, len(riders)),
        grid=(rows // tm,),
        in_specs=[row_spec(d), row_spec(ycp.shape[1]), row_spec(yat.shape[1]),
                  _resident(wo.shape), _resident((1, d)),
                  _resident((d, f)), _resident((d, f)), _resident((f, d)), _resident((1, d))] + r_in,
        out_specs=[row_spec(d)] + r_out,
        out_shape=[jax.ShapeDtypeStruct((rows, d), F32)] + r_shapes,
        compiler_params=pltpu.CompilerParams(dimension_semantics=("arbitrary",),
                                             vmem_limit_bytes=VMEM_LIMIT_BYTES),
        name="out_ffn2",
    )(x1, ycp, yat, wo, g2, wg, wu, wd, gp, *riders)
    return outs[0], list(outs[1:])


def _pool_block_diag(pool_w):
    g, cg, _ = pool_w.shape
    out = jnp.zeros((g * cg, g * cg), pool_w.dtype)
    for k in range(g):
        out = out.at[k * cg:(k + 1) * cg, k * cg:(k + 1) * cg].set(pool_w[k])
    return out


def kernel(x, ffn1_norm, ffn1_w_gate, ffn1_w_up, ffn1_w_down, mix_norm, w_in, conv_dw, conv_dw_bias, conv_ln_gain, conv_ln_bias, pool_w, pool_scale, q_norm, k_norm, lambda_q1, lambda_k1, lambda_q2, lambda_k2, attn_subln, w_out, ffn2_norm, ffn2_w_gate, ffn2_w_up, ffn2_w_down, post_norm):
    b, seq, d = x.shape
    depth = w_in.shape[0]
    rows = b * seq
    d_cp = D_CONV + D_POOL
    row = lambda v: v.reshape(1, -1).astype(F32)
    xf = x.reshape(rows, d).astype(F32)
    stage1 = [w[0].astype(BF16) for w in (ffn1_w_gate, ffn1_w_up, ffn1_w_down, w_in)]
    for l in range(depth):
        lambda_init = 0.8 - 0.6 * math.exp(-0.3 * l)
        pair = lambda g: jnp.concatenate([g, g]).reshape(1, -1).astype(F32)
        x1, ucp, qT, ka, vT, stage3 = _ffn1_win(
            xf, row(ffn1_norm[l]), stage1[0], stage1[1], stage1[2], row(mix_norm[l]), stage1[3],
            pair(q_norm[l]), pair(k_norm[l]), seq,
            [ffn2_w_gate, ffn2_w_up, ffn2_w_down, w_out], l)
        ycp = _convpool(
            ucp.reshape(b, seq, -1), conv_dw[l].astype(F32), row(conv_dw_bias[l]),
            row(conv_ln_gain[l]), row(conv_ln_bias[l]),
            _pool_block_diag(pool_w[l]).astype(BF16), row(pool_scale[l]))
        yat = _diff_attention(
            qT, ka, vT, q_norm[l], k_norm[l], row(lambda_q1[l]), row(lambda_k1[l]), row(lambda_q2[l]),
            row(lambda_k2[l]), attn_subln[l].reshape(-1, 1).astype(F32), lambda_init)
        nxt = l + 1
        xf, stage1 = _out_ffn2(
            x1, ycp.reshape(rows, d_cp), yat.reshape(rows, -1), stage3[3],
            row(ffn2_norm[l]), stage3[0], stage3[1], stage3[2], row(post_norm[l]),
            [ffn1_w_gate, ffn1_w_up, ffn1_w_down, w_in] if nxt < depth else [], nxt)
    return xf.reshape(b, seq, d).astype(x.dtype)
```
